```python
import math
import numpy as np
import jax
import jax.numpy as jnp
from jax import lax

D_MODEL = 2048
BATCH = 16
SEQ = 2048
DEPTH = 4

HEAD_DIM = 128
CONV_DIM = D_MODEL // 4
CONV_GROUPS = CONV_DIM // HEAD_DIM
GDN_HEADS = (D_MODEL - CONV_DIM) // (2 * HEAD_DIM)
NSA_HEADS = GDN_HEADS
GDN_DIM = GDN_HEADS * HEAD_DIM
NSA_DIM = NSA_HEADS * HEAD_DIM
NSA_KV_HEADS = 2
NSA_GROUP = NSA_HEADS // NSA_KV_HEADS
NSA_KV_DIM = 6 * NSA_KV_HEADS * HEAD_DIM
GDN_CONV_W = 4
GDN_CHUNK = 64
L_CMP = 32
D_CMP = 16
L_SLC = 64
TOP_N = 8
WINDOW = 512
Q_BLOCK = 128
SEL_Q_BLOCK = 64
NUM_BUCKETS = 32
MAX_DISTANCE = 128
SHORT_CONV_W = 3
D_FF = ((8 * D_MODEL // 3 + 255) // 256) * 256
PROJ_SIZES = (3 * GDN_DIM, GDN_DIM, GDN_HEADS, GDN_HEADS, NSA_DIM, NSA_KV_DIM, 3 * NSA_HEADS, CONV_DIM, CONV_DIM, CONV_DIM)
PROJ_DIM = sum(PROJ_SIZES)
RMS_EPS = 1e-6
FORCED_SCORE = 1e4

kernel_name = "hybrid_gdn_nsa_shortconv_trunk"


def rms_norm(x, g):
    xf = x.astype(jnp.float32)
    y = xf * lax.rsqrt(jnp.mean(xf * xf, axis=-1, keepdims=True) + RMS_EPS)
    return (y * g.astype(jnp.float32)).astype(x.dtype)


def l2_norm(x):
    return x * lax.rsqrt(jnp.sum(x * x, axis=-1, keepdims=True) + RMS_EPS)


def causal_depthwise_conv(x, w):
    S = x.shape[1]
    K = w.shape[-1]
    xp = jnp.pad(x, ((0, 0), (K - 1, 0), (0, 0)))
    y = xp[:, 0:S, :] * w[:, 0]
    for j in range(1, K):
        y = y + xp[:, j:j + S, :] * w[:, j]
    return y


def masked_softmax(logits, mask):
    logits = jnp.where(mask, logits, -jnp.inf)
    m = jnp.max(logits, axis=-1, keepdims=True)
    m = jnp.where(jnp.isfinite(m), m, 0.0)
    e = jnp.exp(logits - m)
    den = jnp.sum(e, axis=-1, keepdims=True)
    return e / jnp.where(den > 0, den, 1.0)


def t5_bucket(dist):
    n = jnp.maximum(dist, 0)
    max_exact = NUM_BUCKETS // 2
    nf = jnp.maximum(n, 1).astype(jnp.float32)
    large = max_exact + (jnp.log(nf / max_exact) / math.log(MAX_DISTANCE / max_exact) * (NUM_BUCKETS - max_exact)).astype(jnp.int32)
    large = jnp.minimum(large, NUM_BUCKETS - 1)
    return jnp.where(n < max_exact, n, large)


def head_bias(rel_bias, dist):
    b = rel_bias[t5_bucket(dist)].astype(jnp.float32)
    b = jnp.moveaxis(b, -1, 0)
    return b.reshape((NSA_KV_HEADS, NSA_GROUP) + dist.shape)


def gated_delta_rule_chunked(q, k, v, g, beta):
    B, H, S, dk = q.shape
    dv = v.shape[-1]
    C = GDN_CHUNK
    N = S // C
    q = q.reshape(B, H, N, C, dk)
    k = k.reshape(B, H, N, C, dk)
    v = v.reshape(B, H, N, C, dv)
    g = g.reshape(B, H, N, C)
    beta = beta.reshape(B, H, N, C)
    gc = jnp.cumsum(g, axis=-1)
    tril = jnp.asarray(np.tril(np.ones((C, C), dtype=bool)))
    strict = jnp.asarray(np.tril(np.ones((C, C), dtype=bool), -1))
    eye = jnp.eye(C, dtype=jnp.float32)
    diff = gc[..., :, None] - gc[..., None, :]
    decay = jnp.where(tril, jnp.exp(jnp.where(tril, diff, 0.0)), 0.0)
    kb = k * beta[..., None]
    vb = v * beta[..., None]
    L = jnp.where(strict, jnp.einsum('bhncd,bhnsd->bhncs', kb, k) * decay, 0.0)
    rhs = jnp.concatenate([vb, kb * jnp.exp(gc)[..., None]], axis=-1)
    sol = lax.linalg.triangular_solve(L + eye, rhs, left_side=True, lower=True, unit_diagonal=True)
    u = sol[..., :dv]
    w = sol[..., dv:]
    attn = jnp.where(tril, jnp.einsum('bhncd,bhnsd->bhncs', q, k) * decay, 0.0)

    def step(state, inp):
        q_i, k_i, u_i, w_i, gc_i, attn_i = inp
        v_new = u_i - jnp.einsum('bhcd,bhde->bhce', w_i, state)
        o = jnp.einsum('bhcd,bhde->bhce', q_i * jnp.exp(gc_i)[..., None], state) + jnp.einsum('bhcs,bhse->bhce', attn_i, v_new)
        g_last = gc_i[..., -1]
        k_dec = k_i * jnp.exp(g_last[..., None] - gc_i)[..., None]
        state = state * jnp.exp(g_last)[..., None, None] + jnp.einsum('bhcd,bhce->bhde', k_dec, v_new)
        return state, o

    xs = tuple(jnp.moveaxis(t, 2, 0) for t in (q, k, u, w, gc, attn))
    state0 = jnp.zeros((B, H, dk, dv), jnp.float32)
    _, o = lax.scan(step, state0, xs)
    return jnp.moveaxis(o, 0, 2).reshape(B, H, S, dv)


def gdn_mixer(qkv, z, b, a, conv_w, a_log, dt_bias, norm_g):
    B, S, _ = qkv.shape
    H, dk = GDN_HEADS, HEAD_DIM
    qkv = jax.nn.silu(causal_depthwise_conv(qkv, conv_w))
    q, k, v = [t.reshape(B, S, H, dk).transpose(0, 2, 1, 3).astype(jnp.float32) for t in jnp.split(qkv, 3, axis=-1)]
    q = l2_norm(q) * (dk ** -0.5)
    k = l2_norm(k)
    beta = jax.nn.sigmoid(b.astype(jnp.float32)).transpose(0, 2, 1)
    g = (-jnp.exp(a_log.astype(jnp.float32)) * jax.nn.softplus(a.astype(jnp.float32) + dt_bias.astype(jnp.float32))).transpose(0, 2, 1)
    o = gated_delta_rule_chunked(q, k, v, g, beta)
    o = o.transpose(0, 2, 1, 3).astype(z.dtype)
    o = rms_norm(o, norm_g) * jax.nn.silu(z.reshape(B, S, H, dk))
    return o.reshape(B, S, H * dk)


def nsa_mixer(q, kv, gates, q_norm, k_norm, cmp_pos, cmp_w1, cmp_w2, rel_bias):
    B, S, _ = q.shape
    H, Hkv, G, dk = NSA_HEADS, NSA_KV_HEADS, NSA_GROUP, HEAD_DIM
    scale = dk ** -0.5
    t = jnp.arange(S, dtype=jnp.int32)
    q = rms_norm(q.reshape(B, S, H, dk), q_norm)
    q = q.reshape(B, S, Hkv, G, dk).transpose(0, 2, 3, 1, 4)
    kv = kv.reshape(B, S, 6, Hkv, dk).transpose(2, 0, 3, 1, 4)
    k_cmp_tok, v_cmp_tok = kv[0], kv[1]
    k_slc, v_slc = rms_norm(kv[2], k_norm[1]), kv[3]
    k_win, v_win = rms_norm(kv[4], k_norm[2]), kv[5]

    n_cmp = (S - L_CMP) // D_CMP + 1
    cmp_start = np.arange(n_cmp) * D_CMP
    cmp_idx = cmp_start[:, None] + np.arange(L_CMP)[None, :]
    cmp_end = jnp.asarray(cmp_start + L_CMP - 1, dtype=jnp.int32)

    def compress(tok, pe, w1, w2):
        blk = tok[:, :, cmp_idx] + pe
        return jax.nn.silu(blk.reshape(B, Hkv, n_cmp, L_CMP * dk) @ w1) @ w2

    kc = rms_norm(compress(k_cmp_tok, cmp_pos[0], cmp_w1[0], cmp_w2[0]), k_norm[0])
    vc = compress(v_cmp_tok, cmp_pos[1], cmp_w1[1], cmp_w2[1])
    dist_c = t[:, None] - cmp_end[None, :]
    s_c = jnp.einsum('bhgtd,bhnd->bhgtn', q, kc).astype(jnp.float32) * scale + head_bias(rel_bias, dist_c)
    p_c = masked_softmax(s_c, dist_c >= 0)
    o_cmp = jnp.einsum('bhgtn,bhnd->bhgtd', p_c.astype(vc.dtype), vc)

    n_sel = S // L_SLC
    top_n = min(TOP_N, n_sel)
    sel_start = np.arange(n_sel) * L_SLC
    overlap = ((cmp_start[:, None] < sel_start[None, :] + L_SLC) & (cmp_start[:, None] + L_CMP > sel_start[None, :])).astype(np.float32)
    imp = jnp.einsum('bhgtn,nj->bhtj', p_c, jnp.asarray(overlap))
    cur = t // L_SLC
    j = jnp.arange(n_sel, dtype=jnp.int32)
    future = j[None, :] > cur[:, None]
    forced = (j[None, :] == 0) | (j[None, :] == cur[:, None]) | (j[None, :] == cur[:, None] - 1)
    imp = jnp.where(future, -1.0, jnp.where(forced, FORCED_SCORE, imp))
    _, sel_idx = lax.top_k(imp, top_n)
    sel_idx = sel_idx.astype(jnp.int32)

    nq = S // SEL_Q_BLOCK
    kb = k_slc.reshape(B * Hkv * n_sel, L_SLC, dk)
    vb = v_slc.reshape(B * Hkv * n_sel, L_SLC, dk)
    base = (jnp.arange(B * Hkv, dtype=jnp.int32) * n_sel).reshape(B, Hkv, 1, 1)
    flat = base + sel_idx
    q_ch = q.reshape(B, Hkv, G, nq, SEL_Q_BLOCK, dk).transpose(3, 0, 1, 2, 4, 5)
    flat_ch = flat.reshape(B, Hkv, nq, SEL_Q_BLOCK, top_n).transpose(2, 0, 1, 3, 4)
    sel_ch = sel_idx.reshape(B, Hkv, nq, SEL_Q_BLOCK, top_n).transpose(2, 0, 1, 3, 4)
    t_ch = t.reshape(nq, SEL_Q_BLOCK)
    tab = rel_bias.T.reshape(Hkv, G, NUM_BUCKETS)
    M = top_n * L_SLC

    def bias_lookup(tab_h, bk_h):
        return jnp.moveaxis(tab_h[:, bk_h], 0, 1)

    def sel_block(args):
        qc, fc, sc, tc = args
        kg = kb[fc].reshape(B, Hkv, SEL_Q_BLOCK, M, dk)
        vg = vb[fc].reshape(B, Hkv, SEL_Q_BLOCK, M, dk)
        pos = (sc[..., None] * L_SLC + jnp.arange(L_SLC, dtype=jnp.int32)).reshape(B, Hkv, SEL_Q_BLOCK, M)
        dist = tc[None, None, :, None] - pos
        bias = jax.vmap(bias_lookup, in_axes=(0, 1), out_axes=1)(tab, t5_bucket(dist))
        s = jnp.einsum('bhgqd,bhqmd->bhgqm', qc, kg).astype(jnp.float32) * scale + bias.astype(jnp.float32)
        p = masked_softmax(s, (dist >= 0)[:, :, None])
        return jnp.einsum('bhgqm,bhqmd->bhgqd', p.astype(vg.dtype), vg)

    o_slc = lax.map(sel_block, (q_ch, flat_ch, sel_ch, t_ch))
    o_slc = o_slc.transpose(1, 2, 3, 0, 4, 5).reshape(B, Hkv, G, S, dk)

    nb = S // Q_BLOCK
    span = Q_BLOCK + WINDOW
    band_idx = np.arange(nb)[:, None] * Q_BLOCK + np.arange(span)[None, :]
    pad = ((0, 0), (0, 0), (WINDOW, 0), (0, 0))
    kw = jnp.pad(k_win, pad)[:, :, band_idx]
    vw = jnp.pad(v_win, pad)[:, :, band_idx]
    dist_w = jnp.asarray(WINDOW + np.arange(Q_BLOCK)[:, None] - np.arange(span)[None, :], dtype=jnp.int32)
    key_pos = jnp.asarray(band_idx - WINDOW)
    mask_w = ((dist_w >= 0) & (dist_w < WINDOW))[None] & (key_pos >= 0)[:, None, :]
    bias_w = head_bias(rel_bias, dist_w)[:, :, None]
    qw = q.reshape(B, Hkv, G, nb, Q_BLOCK, dk)
    s_w = jnp.einsum('bhgiqd,bhikd->bhgiqk', qw, kw).astype(jnp.float32) * scale + bias_w
    p_w = masked_softmax(s_w, mask_w)
    o_win = jnp.einsum('bhgiqk,bhikd->bhgiqd', p_w.astype(vw.dtype), vw).reshape(B, Hkv, G, S, dk)

    gt = jax.nn.sigmoid(gates.reshape(B, S, 3, Hkv, G)).transpose(2, 0, 3, 4, 1)[..., None]
    o = gt[0] * o_cmp + gt[1] * o_slc + gt[2] * o_win
    return o.transpose(0, 3, 1, 2, 4).reshape(B, S, H * dk)


def short_conv_mixer(u, b, c, w):
    return b * causal_depthwise_conv(c * u, w)


def setup_inputs(seed: int = 0) -> dict:
    key = jax.random.key(seed)
    ks = jax.random.split(key, 20)
    f32 = jnp.float32

    def nrm(k, shape, fan_in):
        return jax.random.normal(k, shape, f32) * (fan_in ** -0.5)

    def gain(k, shape):
        return 1.0 + 0.01 * jax.random.normal(k, shape, f32)

    x = jax.random.normal(ks[0], (BATCH, SEQ, D_MODEL), f32)
    rel_bias = 0.2 * jax.random.normal(ks[1], (NUM_BUCKETS, NSA_HEADS), f32)
    norm_mix = gain(ks[2], (DEPTH, D_MODEL))
    w_in = nrm(ks[3], (DEPTH, D_MODEL, PROJ_DIM), D_MODEL)
    gdn_conv = nrm(ks[4], (DEPTH, 3 * GDN_DIM, GDN_CONV_W), GDN_CONV_W)
    gdn_a_log = jnp.log(jax.random.uniform(ks[5], (DEPTH, GDN_HEADS), f32, 1.0, 16.0))
    dt = jnp.exp(jax.random.uniform(ks[6], (DEPTH, GDN_HEADS), f32, math.log(1e-3), math.log(1e-1)))
    gdn_dt_bias = dt + jnp.log(-jnp.expm1(-dt))
    gdn_norm = gain(ks[7], (DEPTH, HEAD_DIM))
    nsa_q_norm = gain(ks[8], (DEPTH, HEAD_DIM))
    nsa_k_norm = gain(ks[9], (DEPTH, 3, HEAD_DIM))
    cmp_pos = 0.02 * jax.random.normal(ks[10], (DEPTH, 2, L_CMP, HEAD_DIM), f32)
    cmp_w1 = nrm(ks[11], (DEPTH, 2, L_CMP * HEAD_DIM, HEAD_DIM), L_CMP * HEAD_DIM)
    cmp_w2 = nrm(ks[12], (DEPTH, 2, HEAD_DIM, HEAD_DIM), HEAD_DIM)
    sconv_w = nrm(ks[13], (DEPTH, CONV_DIM, SHORT_CONV_W), SHORT_CONV_W)
    w_out = nrm(ks[14], (DEPTH, D_MODEL, D_MODEL), D_MODEL)
    norm_ffn = gain(ks[15], (DEPTH, D_MODEL))
    w_gate = nrm(ks[16], (DEPTH, D_MODEL, D_FF), D_MODEL)
    w_up = nrm(ks[17], (DEPTH, D_MODEL, D_FF), D_MODEL)
    w_down = nrm(ks[18], (DEPTH, D_FF, D_MODEL), D_FF)
    return {"x": x, "rel_bias": rel_bias, "norm_mix": norm_mix, "w_in": w_in, "gdn_conv": gdn_conv,
            "gdn_a_log": gdn_a_log, "gdn_dt_bias": gdn_dt_bias, "gdn_norm": gdn_norm,
            "nsa_q_norm": nsa_q_norm, "nsa_k_norm": nsa_k_norm, "cmp_pos": cmp_pos, "cmp_w1": cmp_w1,
            "cmp_w2": cmp_w2, "sconv_w": sconv_w, "w_out": w_out, "norm_ffn": norm_ffn,
            "w_gate": w_gate, "w_up": w_up, "w_down": w_down}


def reference(x, rel_bias, norm_mix, w_in, gdn_conv, gdn_a_log, gdn_dt_bias, gdn_norm, nsa_q_norm, nsa_k_norm,
              cmp_pos, cmp_w1, cmp_w2, sconv_w, w_out, norm_ffn, w_gate, w_up, w_down):
    splits = np.cumsum(PROJ_SIZES)[:-1].tolist()
    for l in range(DEPTH):
        h = rms_norm(x, norm_mix[l])
        proj = h @ w_in[l]
        (gdn_qkv, gdn_z, gdn_b, gdn_a, nsa_q, nsa_kv, nsa_g, cu, cb, cc) = jnp.split(proj, splits, axis=-1)
        y_gdn = gdn_mixer(gdn_qkv, gdn_z, gdn_b, gdn_a, gdn_conv[l], gdn_a_log[l], gdn_dt_bias[l], gdn_norm[l])
        y_nsa = nsa_mixer(nsa_q, nsa_kv, nsa_g, nsa_q_norm[l], nsa_k_norm[l], cmp_pos[l], cmp_w1[l], cmp_w2[l], rel_bias)
        y_conv = short_conv_mixer(cu, cb, cc, sconv_w[l])
        x = x + jnp.concatenate([y_gdn, y_nsa, y_conv], axis=-1) @ w_out[l]
        h = rms_norm(x, norm_ffn[l])
        x = x + (jax.nn.silu(h @ w_gate[l]) * (h @ w_up[l])) @ w_down[l]
    return x
```

```python
import functools
import math

import numpy as np
import jax
import jax.numpy as jnp
from jax import lax
from jax.experimental import pallas as pl
from jax.experimental.pallas import tpu as pltpu

HEAD_DIM = 128
GDN_HEADS = 6
NSA_HEADS = 6
NSA_KV_HEADS = 2
NSA_GROUP = NSA_HEADS // NSA_KV_HEADS
CONV_DIM = 512
GDN_DIM = GDN_HEADS * HEAD_DIM
NSA_DIM = NSA_HEADS * HEAD_DIM
GDN_CONV_W = 4
GDN_CHUNK = 64
L_CMP = 32
D_CMP = 16
L_SLC = 64
TOP_N = 8
WINDOW = 512
NUM_BUCKETS = 32
MAX_DISTANCE = 128
SHORT_CONV_W = 3
RMS_EPS = 1e-6
FORCED_SCORE = 1e4
NEG_BIG = -1e30

LANES = 128
SUBLANES = 8
VMEM_LIMIT_BYTES = 56 * 1024 * 1024

BLK_QKV = 0
BLK_Z = 18
BLK_NQ = 24
BLK_NKV = 30
BLK_SMALL = 42
BLK_CU = 44
BLK_CB = 48
BLK_CC = 52
N_BLK = 56
PROJ_PAD = N_BLK * LANES
LANE_BETA = 0
LANE_A = GDN_HEADS
LANE_GATE = 2 * GDN_HEADS

GDN_GROUP_ROWS = 256
NSA_TQ = 128
NSA_TK = 128
N_CMP_PAD = 128

F32 = jnp.float32
BF16 = jnp.bfloat16


def _dot(a, b):
    return jnp.dot(a, b, preferred_element_type=F32)


def _dot_nt(a, b):
    return lax.dot_general(a, b, (((1,), (1,)), ((), ())), preferred_element_type=F32)


def _dot_tn(a, b):
    return lax.dot_general(a, b, (((0,), (0,)), ((), ())), preferred_element_type=F32)


def _silu(x):
    return x * jax.nn.sigmoid(x)


def _softplus(x):
    return jnp.maximum(x, 0.0) + jnp.log1p(jnp.exp(-jnp.abs(x)))


def _lane_pick(x, lane):
    idx = lax.broadcasted_iota(jnp.int32, x.shape, 1)
    return jnp.sum(jnp.where(idx == lane, x, 0.0), axis=-1, keepdims=True)


def _shift_rows(cur, prev, s):
    rolled = pltpu.roll(cur, s, 0)
    rows = lax.broadcasted_iota(jnp.int32, cur.shape, 0)
    out = rolled
    for r in range(s):
        out = jnp.where(rows == r, prev[SUBLANES - s + r:SUBLANES - s + r + 1, :], out)
    return out


def _inproj_kernel(x_ref, g_ref, w_ref, o_ref, h_ref):
    @pl.when(pl.program_id(1) == 0)
    def _():
        x = x_ref[...]
        ms = jnp.mean(x * x, axis=-1, keepdims=True)
        h_ref[...] = (x * lax.rsqrt(ms + RMS_EPS) * g_ref[...]).astype(BF16)

    o_ref[...] = _dot(h_ref[...], w_ref[...])


def _inproj(x2, g, w, tm=512, tn=512):
    T, D = x2.shape
    N = w.shape[1]
    return pl.pallas_call(
        _inproj_kernel,
        grid=(T // tm, N // tn),
        in_specs=[
            pl.BlockSpec((tm, D), lambda i, j: (i, 0)),
            pl.BlockSpec((1, D), lambda i, j: (0, 0)),
            pl.BlockSpec((D, tn), lambda i, j: (0, j)),
        ],
        out_specs=pl.BlockSpec((tm, tn), lambda i, j: (i, j)),
        out_shape=jax.ShapeDtypeStruct((T, N), F32),
        scratch_shapes=[pltpu.VMEM((tm, D), BF16)],
        compiler_params=pltpu.CompilerParams(
            dimension_semantics=("parallel", "arbitrary"), vmem_limit_bytes=VMEM_LIMIT_BYTES),
        name="inproj",
    )(x2, g, w)


def _gdn_kernel(q_ref, k_ref, v_ref, z_ref, small_ref, cq_ref, ck_ref, cv_ref, alog_ref, dt_ref,
                ng_ref, o_ref, *, seq):
    h = pl.program_id(1)
    R = GDN_GROUP_ROWS
    C = GDN_CHUNK
    n_groups = seq // R
    rows = lax.broadcasted_iota(jnp.int32, (R, R), 0)
    cols = lax.broadcasted_iota(jnp.int32, (R, R), 1)
    same = (rows & -C) == (cols & -C)
    tril = same & (rows >= cols)
    strict = same & (rows > cols)
    row_in_chunk = lax.broadcasted_iota(jnp.int32, (R, LANES), 0) & (C - 1)
    neg_a = -jnp.exp(alog_ref[...])

    def conv_silu(ref, w_ref, r0, first):
        cur = ref[pl.ds(r0, R), :]
        prev = ref[pl.ds(jnp.maximum(r0 - SUBLANES, 0), SUBLANES), :]
        prev = jnp.where(first, 0.0, prev)
        w = w_ref[...]
        y = cur * w[3:4, :]
        for s in range(1, GDN_CONV_W):
            y = y + _shift_rows(cur, prev, s) * w[3 - s:4 - s, :]
        return _silu(y)

    def group(gi, state):
        r0 = pl.multiple_of(gi * R, R)
        first = gi == 0
        q = conv_silu(q_ref, cq_ref, r0, first)
        k = conv_silu(k_ref, ck_ref, r0, first)
        v = conv_silu(v_ref, cv_ref, r0, first)
        q = q * lax.rsqrt(jnp.sum(q * q, axis=-1, keepdims=True) + RMS_EPS) * (HEAD_DIM ** -0.5)
        k = k * lax.rsqrt(jnp.sum(k * k, axis=-1, keepdims=True) + RMS_EPS)
        small = small_ref[pl.ds(r0, R), :]
        beta = _lane_pick(jax.nn.sigmoid(small), LANE_BETA + h)
        g = _lane_pick(neg_a * _softplus(small + dt_ref[...]), LANE_A + h)
        gc = jnp.broadcast_to(g, (R, LANES))
        s = 1
        while s < C:
            gc = gc + jnp.where(row_in_chunk >= s, pltpu.roll(gc, s, 0), 0.0)
            s *= 2
        gc_col = jnp.concatenate([gc, gc], axis=1)
        gc_row = gc_col.T
        decay = jnp.where(tril, jnp.exp(jnp.where(tril, gc_col - gc_row, 0.0)), 0.0)
        kb = k * beta
        vb = v * beta
        k16 = k.astype(BF16)
        lmat = jnp.where(strict, _dot_nt(kb.astype(BF16), k16) * decay, 0.0)
        y = -lmat
        p = lmat
        n = 1
        while 2 * n < C:
            p16 = p.astype(BF16)
            p = _dot(p16, p16)
            n *= 2
            y = y + p + _dot(y.astype(BF16), p.astype(BF16))
        eg = jnp.exp(gc)
        rhs = jnp.concatenate([vb, kb * eg], axis=1)
        sol = rhs + _dot(y.astype(BF16), rhs.astype(BF16))
        u = sol[:, :HEAD_DIM]
        w = sol[:, HEAD_DIM:]
        attn = jnp.where(tril, _dot_nt(q.astype(BF16), k16) * decay, 0.0)
        qg = (q * eg).astype(BF16)
        outs = []
        for c in range(R // C):
            sl = slice(c * C, (c + 1) * C)
            g_last = gc[c * C + C - 1:c * C + C, :]
            kdec = k[sl] * jnp.exp(g_last - gc[sl])
            s16 = state.astype(BF16)
            v_new = u[sl] - _dot(w[sl].astype(BF16), s16)
            vn16 = v_new.astype(BF16)
            o_c = _dot(qg[sl], s16) + _dot(attn[sl, sl].astype(BF16), vn16)
            state = state * jnp.exp(g_last[:, 0:1]) + _dot_tn(kdec.astype(BF16), vn16)
            outs.append(o_c)
        o = jnp.concatenate(outs, axis=0)
        o = o * lax.rsqrt(jnp.mean(o * o, axis=-1, keepdims=True) + RMS_EPS) * ng_ref[...]
        o_ref[pl.ds(r0, R), :] = (o * _silu(z_ref[pl.ds(r0, R), :])).astype(o_ref.dtype)
        return state

    lax.fori_loop(0, n_groups, group, jnp.zeros((HEAD_DIM, HEAD_DIM), F32))


def _gdn(proj, conv_t, alog_vec, dt_vec, norm_g, batch, seq):
    T = proj.shape[0]
    H = GDN_HEADS
    col = lambda base: (lambda b, h: (b, base + h))
    wcol = lambda base: (lambda b, h: (0, base + h))
    const = lambda b, h: (0, 0)
    return pl.pallas_call(
        functools.partial(_gdn_kernel, seq=seq),
        grid=(batch, H),
        in_specs=[
            pl.BlockSpec((seq, LANES), col(BLK_QKV)),
            pl.BlockSpec((seq, LANES), col(BLK_QKV + H)),
            pl.BlockSpec((seq, LANES), col(BLK_QKV + 2 * H)),
            pl.BlockSpec((seq, LANES), col(BLK_Z)),
            pl.BlockSpec((seq, LANES), lambda b, h: (b, BLK_SMALL)),
            pl.BlockSpec((GDN_CONV_W, LANES), wcol(0)),
            pl.BlockSpec((GDN_CONV_W, LANES), wcol(H)),
            pl.BlockSpec((GDN_CONV_W, LANES), wcol(2 * H)),
            pl.BlockSpec((1, LANES), const),
            pl.BlockSpec((1, LANES), const),
            pl.BlockSpec((1, LANES), const),
        ],
        out_specs=pl.BlockSpec((seq, LANES), lambda b, h: (b, h)),
        out_shape=jax.ShapeDtypeStruct((T, GDN_DIM), BF16),
        compiler_params=pltpu.CompilerParams(
            dimension_semantics=("parallel", "parallel"), vmem_limit_bytes=VMEM_LIMIT_BYTES),
        name="gdn",
    )(proj, proj, proj, proj, proj, conv_t, conv_t, conv_t, alog_vec, dt_vec, norm_g)


def _cmp_kernel(kt_ref, vt_ref, pos_ref, w1_ref, w2_ref, kn_ref, kc_ref, vc_ref, *, seq):
    n_seg = seq // D_CMP
    out_refs = (kc_ref, vc_ref)
    for which, tok_ref in enumerate((kt_ref, vt_ref)):
        hi = jnp.zeros((n_seg, HEAD_DIM), F32)
        lo = jnp.zeros((n_seg, HEAD_DIM), F32)
        for p in range(D_CMP):
            seg = tok_ref[pl.ds(p, n_seg, stride=D_CMP), :]
            a = (seg + pos_ref[which, p:p + 1, :]).astype(BF16)
            hi = hi + _dot(a, w1_ref[which, p * HEAD_DIM:(p + 1) * HEAD_DIM, :])
            b = (seg + pos_ref[which, D_CMP + p:D_CMP + p + 1, :]).astype(BF16)
            lo = lo + _dot(b, w1_ref[which, (D_CMP + p) * HEAD_DIM:(D_CMP + p + 1) * HEAD_DIM, :])
        hid = hi + pltpu.roll(lo, n_seg - 1, 0)
        out = _dot(_silu(hid).astype(BF16), w2_ref[which])
        if which == 0:
            out = out * lax.rsqrt(jnp.mean(out * out, axis=-1, keepdims=True) + RMS_EPS) * kn_ref[...]
        out_refs[which][...] = out.astype(BF16)


def _compress(proj, cmp_pos, w1, w2, kn0, batch, seq):
    n_seg = seq // D_CMP
    shp = jax.ShapeDtypeStruct((batch, NSA_KV_HEADS, n_seg, HEAD_DIM), BF16)
    const3 = lambda b, h: (0, 0, 0)
    return pl.pallas_call(
        functools.partial(_cmp_kernel, seq=seq),
        grid=(batch, NSA_KV_HEADS),
        in_specs=[
            pl.BlockSpec((seq, LANES), lambda b, h: (b, BLK_NKV + h)),
            pl.BlockSpec((seq, LANES), lambda b, h: (b, BLK_NKV + NSA_KV_HEADS + h)),
            pl.BlockSpec((2, L_CMP, HEAD_DIM), const3),
            pl.BlockSpec((2, L_CMP * HEAD_DIM, HEAD_DIM), const3),
            pl.BlockSpec((2, HEAD_DIM, HEAD_DIM), const3),
            pl.BlockSpec((1, HEAD_DIM), lambda b, h: (0, 0)),
        ],
        out_specs=[pl.BlockSpec((None, None, n_seg, HEAD_DIM), lambda b, h: (b, h, 0, 0))] * 2,
        out_shape=[shp, shp],
        compiler_params=pltpu.CompilerParams(
            dimension_semantics=("parallel", "parallel"), vmem_limit_bytes=VMEM_LIMIT_BYTES),
        name="nsa_compress",
    )(proj, proj, cmp_pos, w1, w2, kn0)


def _nsa_kernel(q_ref, kc_ref, vc_ref, ks_ref, vs_ref, kw_ref, vw_ref, small_ref, qn_ref, kn_ref,
                bt_ref, bc_ref, ovt_ref, exp_ref, o_ref,
                ksn_ref, vsb_ref, kwn_ref, vwb_ref, mask_ref, *, seq):
    hkv = pl.program_id(1)
    qt = pl.program_id(2)
    TQ, TK, G = NSA_TQ, NSA_TK, NSA_GROUP
    n_kt = seq // TK
    RQ = G * TQ

    @pl.when(qt == 0)
    def _():
        def norm_rows(i, carry):
            r0 = pl.multiple_of(i * 256, 256)
            for src, dst, gi in ((ks_ref, ksn_ref, 0), (kw_ref, kwn_ref, 1)):
                x = src[pl.ds(r0, 256), :]
                xn = x * lax.rsqrt(jnp.mean(x * x, axis=-1, keepdims=True) + RMS_EPS) * kn_ref[gi:gi + 1, :]
                dst[pl.ds(r0, 256), :] = xn.astype(BF16)
            vsb_ref[pl.ds(r0, 256), :] = vs_ref[pl.ds(r0, 256), :].astype(BF16)
            vwb_ref[pl.ds(r0, 256), :] = vw_ref[pl.ds(r0, 256), :].astype(BF16)
            return carry
        lax.fori_loop(0, seq // 256, norm_rows, 0)

    scale = HEAD_DIM ** -0.5
    qs = []
    for g in range(G):
        x = q_ref[:, g * HEAD_DIM:(g + 1) * HEAD_DIM]
        xn = x * lax.rsqrt(jnp.mean(x * x, axis=-1, keepdims=True) + RMS_EPS) * qn_ref[...]
        qs.append((xn * scale).astype(BF16))
    q = jnp.concatenate(qs, axis=0)

    row = lax.broadcasted_iota(jnp.int32, (RQ, TK), 0) & (TQ - 1)
    col = lax.broadcasted_iota(jnp.int32, (RQ, TK), 1)
    t_row = qt * TQ + row

    s = _dot_nt(q, kc_ref[...]) + bc_ref[...]
    n_cmp = (seq - L_CMP) // D_CMP + 1
    cmp_end = jnp.where(col < n_cmp, col * D_CMP + (L_CMP - 1), seq)
    valid = t_row >= cmp_end
    s = jnp.where(valid, s, NEG_BIG)
    m = jnp.max(s, axis=-1, keepdims=True)
    p = jnp.where(valid, jnp.exp(s - m), 0.0)
    den = jnp.sum(p, axis=-1, keepdims=True)
    p = p / jnp.where(den > 0, den, 1.0)
    o_cmp = _dot(p.astype(BF16), vc_ref[...])

    p_sum = p[0:TQ]
    for g in range(1, G):
        p_sum = p_sum + p[g * TQ:(g + 1) * TQ]
    imp = lax.dot_general(ovt_ref[...], p_sum, (((1,), (1,)), ((), ())),
                          precision=lax.Precision.HIGHEST, preferred_element_type=F32)
    jb = lax.broadcasted_iota(jnp.int32, (LANES, TQ), 0)
    cur = (qt * TQ + lax.broadcasted_iota(jnp.int32, (LANES, TQ), 1)) >> int(math.log2(L_SLC))
    val = jnp.where(jb == 0, FORCED_SCORE, jnp.where(jb >= cur - 1, FORCED_SCORE, imp))
    val = jnp.where(jb > cur, -1.0, val)
    n_sel = seq // L_SLC
    val = jnp.where(jb < n_sel, val, -2.0)
    cnt = jnp.zeros((LANES, TQ), F32)
    for i in range(n_sel):
        r = val[i:i + 1, :]
        cnt = cnt + jnp.where(r > val, 1.0, jnp.where(r == val, jnp.where(jb > i, 1.0, 0.0), 0.0))
    sel = jnp.where(cnt < min(TOP_N, n_sel), 1.0, 0.0).T.astype(BF16)
    for kt in range(n_kt):
        mask_ref[kt] = _dot(sel, exp_ref[:, kt * TK:(kt + 1) * TK])

    def flash_step(carry, k, v, bias, valid):
        m, l, acc = carry
        s = jnp.where(valid, _dot_nt(q, k) + bias, NEG_BIG)
        m_new = jnp.maximum(m, jnp.max(s, axis=-1, keepdims=True))
        alpha = jnp.exp(m - m_new)
        p = jnp.exp(s - m_new)
        l = alpha * l + jnp.sum(p, axis=-1, keepdims=True)
        acc = alpha * acc + _dot(p.astype(BF16), v)
        return m_new, l, acc

    def init():
        return (jnp.full((RQ, 1), NEG_BIG, F32), jnp.zeros((RQ, 1), F32), jnp.zeros((RQ, HEAD_DIM), F32))

    def tile(ref, kt):
        return ref[pl.ds(pl.multiple_of(kt * TK, TK), TK), :]

    def sel_mask(kt):
        mk = mask_ref[kt]
        return jnp.concatenate([mk] * G, axis=0)

    causal = row >= col

    carry = flash_step(init(), tile(ksn_ref, qt), tile(vsb_ref, qt), bt_ref[0],
                       jnp.where(causal, sel_mask(qt), 0.0) > 0.5)
    kt1 = jnp.maximum(qt - 1, 0)
    carry = flash_step(carry, tile(ksn_ref, kt1), tile(vsb_ref, kt1), bt_ref[1],
                       sel_mask(kt1) > jnp.where(qt >= 1, 0.5, 2.0))

    def far_step(kt, carry):
        return flash_step(carry, tile(ksn_ref, kt), tile(vsb_ref, kt), bt_ref[2], sel_mask(kt) > 0.5)

    m, l, acc = lax.fori_loop(0, jnp.maximum(qt - 1, 0), far_step, carry)
    o_slc = acc / l

    n_win = WINDOW // TK
    carry = flash_step(init(), tile(kwn_ref, qt), tile(vwb_ref, qt), bt_ref[0], causal)
    for d in range(1, n_win + 1):
        ktd = jnp.maximum(qt - d, 0)
        if d < n_win:
            inside = col >= jnp.where(qt >= d, 0, TK)
        else:
            inside = (col - row) >= jnp.where(qt >= d, 1, 2 * TK)
        carry = flash_step(carry, tile(kwn_ref, ktd), tile(vwb_ref, ktd), bt_ref[min(d, 2)], inside)
    m, l, acc = carry
    o_win = acc / l

    sig = jax.nn.sigmoid(small_ref[...])
    for g in range(G):
        rs = slice(g * TQ, (g + 1) * TQ)
        lane = LANE_GATE + hkv * G + g
        o = (_lane_pick(sig, lane) * o_cmp[rs]
             + _lane_pick(sig, lane + NSA_HEADS) * o_slc[rs]
             + _lane_pick(sig, lane + 2 * NSA_HEADS) * o_win[rs])
        o_ref[:, g * HEAD_DIM:(g + 1) * HEAD_DIM] = o.astype(o_ref.dtype)


def _nsa(proj, kc, vc, qn, kn12, bias_tiles, bias_cmp, ovt, expand, batch, seq):
    T = proj.shape[0]
    TQ, G = NSA_TQ, NSA_GROUP
    nqt = seq // TQ
    n_kt = seq // NSA_TK
    gw = G * HEAD_DIM
    kvcol = lambda part: (lambda b, h, t: (b, BLK_NKV + part * NSA_KV_HEADS + h))
    return pl.pallas_call(
        functools.partial(_nsa_kernel, seq=seq),
        grid=(batch, NSA_KV_HEADS, nqt),
        in_specs=[
            pl.BlockSpec((TQ, gw), lambda b, h, t: (b * nqt + t, BLK_NQ * LANES // gw + h)),
            pl.BlockSpec((None, None, N_CMP_PAD, HEAD_DIM), lambda b, h, t: (b, h, 0, 0)),
            pl.BlockSpec((None, None, N_CMP_PAD, HEAD_DIM), lambda b, h, t: (b, h, 0, 0)),
            pl.BlockSpec((seq, LANES), kvcol(2)),
            pl.BlockSpec((seq, LANES), kvcol(3)),
            pl.BlockSpec((seq, LANES), kvcol(4)),
            pl.BlockSpec((seq, LANES), kvcol(5)),
            pl.BlockSpec((TQ, LANES), lambda b, h, t: (b * nqt + t, BLK_SMALL)),
            pl.BlockSpec((1, HEAD_DIM), lambda b, h, t: (0, 0)),
            pl.BlockSpec((2, HEAD_DIM), lambda b, h, t: (0, 0)),
            pl.BlockSpec((None, 3, G * TQ, NSA_TK), lambda b, h, t: (h, 0, 0, 0)),
            pl.BlockSpec((None, None, G * TQ, N_CMP_PAD), lambda b, h, t: (h, t, 0, 0)),
            pl.BlockSpec((LANES, N_CMP_PAD), lambda b, h, t: (0, 0)),
            pl.BlockSpec((LANES, seq), lambda b, h, t: (0, 0)),
        ],
        out_specs=pl.BlockSpec((TQ, gw), lambda b, h, t: (b * nqt + t, h)),
        out_shape=jax.ShapeDtypeStruct((T, NSA_DIM), BF16),
        scratch_shapes=[
            pltpu.VMEM((seq, HEAD_DIM), BF16),
            pltpu.VMEM((seq, HEAD_DIM), BF16),
            pltpu.VMEM((seq, HEAD_DIM), BF16),
            pltpu.VMEM((seq, HEAD_DIM), BF16),
            pltpu.VMEM((n_kt, TQ, NSA_TK), F32),
        ],
        compiler_params=pltpu.CompilerParams(
            dimension_semantics=("parallel", "parallel", "arbitrary"), vmem_limit_bytes=VMEM_LIMIT_BYTES),
        name="nsa_attention",
    )(proj, kc, vc, proj, proj, proj, proj, proj, qn, kn12, bias_tiles, bias_cmp, ovt, expand)


def _outproj_kernel(x_ref, yg_ref, yn_ref, cu_ref, cb_ref, cc_ref, cup_ref, ccp_ref, sw_ref,
                    wg_ref, wn_ref, wc_ref, o_ref, *, tiles_per_seq):
    i = pl.program_id(0)
    v = cc_ref[...] * cu_ref[...]
    vp = jnp.where(i % tiles_per_seq == 0, 0.0, ccp_ref[...] * cup_ref[...])
    sw = sw_ref[...]
    y = v * sw[2:3, :]
    for s in range(1, SHORT_CONV_W):
        y = y + _shift_rows(v, vp, s) * sw[2 - s:3 - s, :]
    yc = (cb_ref[...] * y).astype(BF16)
    o_ref[...] = (x_ref[...] + _dot(yg_ref[...], wg_ref[...]) + _dot(yn_ref[...], wn_ref[...])
                  + _dot(yc, wc_ref[...]))


def _outproj(x2, y_gdn, y_nsa, proj, sconv_t, w_out, seq, tm=512):
    T, D = x2.shape
    cb = CONV_DIM // LANES
    prev = lambda blk: (lambda i: (jnp.maximum(i * (tm // SUBLANES) - 1, 0), blk // cb))
    const = lambda i: (0, 0)
    return pl.pallas_call(
        functools.partial(_outproj_kernel, tiles_per_seq=seq // tm),
        grid=(T // tm,),
        in_specs=[
            pl.BlockSpec((tm, D), lambda i: (i, 0)),
            pl.BlockSpec((tm, GDN_DIM), lambda i: (i, 0)),
            pl.BlockSpec((tm, NSA_DIM), lambda i: (i, 0)),
            pl.BlockSpec((tm, CONV_DIM), lambda i: (i, BLK_CU // cb)),
            pl.BlockSpec((tm, CONV_DIM), lambda i: (i, BLK_CB // cb)),
            pl.BlockSpec((tm, CONV_DIM), lambda i: (i, BLK_CC // cb)),
            pl.BlockSpec((SUBLANES, CONV_DIM), prev(BLK_CU)),
            pl.BlockSpec((SUBLANES, CONV_DIM), prev(BLK_CC)),
            pl.BlockSpec((SHORT_CONV_W, CONV_DIM), const),
            pl.BlockSpec((GDN_DIM, D), lambda i: (0, 0)),
            pl.BlockSpec((NSA_DIM, D), lambda i: (GDN_DIM // NSA_DIM, 0)),
            pl.BlockSpec((CONV_DIM, D), lambda i: ((GDN_DIM + NSA_DIM) // CONV_DIM, 0)),
        ],
        out_specs=pl.BlockSpec((tm, D), lambda i: (i, 0)),
        out_shape=jax.ShapeDtypeStruct((T, D), F32),
        compiler_params=pltpu.CompilerParams(
            dimension_semantics=("parallel",), vmem_limit_bytes=VMEM_LIMIT_BYTES),
        name="outproj",
    )(x2, y_gdn, y_nsa, proj, proj, proj, proj, proj, sconv_t, w_out, w_out, w_out)


def _ffn_kernel(x_ref, g_ref, wg_ref, wu_ref, wd_ref, o_ref, h_ref):
    @pl.when(pl.program_id(1) == 0)
    def _():
        x = x_ref[...]
        ms = jnp.mean(x * x, axis=-1, keepdims=True)
        h_ref[...] = (x * lax.rsqrt(ms + RMS_EPS) * g_ref[...]).astype(BF16)
        o_ref[...] = x

    h = h_ref[...]
    a = _dot(h, wg_ref[...])
    b = _dot(h, wu_ref[...])
    o_ref[...] += _dot((_silu(a) * b).astype(BF16), wd_ref[...])


def _ffn(x2, g, w_gate, w_up, w_down, tm=512, tf=512):
    T, D = x2.shape
    F = w_gate.shape[1]
    return pl.pallas_call(
        _ffn_kernel,
        grid=(T // tm, F // tf),
        in_specs=[
            pl.BlockSpec((tm, D), lambda i, j: (i, 0)),
            pl.BlockSpec((1, D), lambda i, j: (0, 0)),
            pl.BlockSpec((D, tf), lambda i, j: (0, j)),
            pl.BlockSpec((D, tf), lambda i, j: (0, j)),
            pl.BlockSpec((tf, D), lambda i, j: (j, 0)),
        ],
        out_specs=pl.BlockSpec((tm, D), lambda i, j: (i, 0)),
        out_shape=jax.ShapeDtypeStruct((T, D), F32),
        scratch_shapes=[pltpu.VMEM((tm, D), BF16)],
        compiler_params=pltpu.CompilerParams(
            dimension_semantics=("parallel", "arbitrary"), vmem_limit_bytes=VMEM_LIMIT_BYTES),
        name="ffn",
    )(x2, g, w_gate, w_up, w_down)


def _t5_bucket_np(dist):
    n = np.maximum(dist, 0)
    max_exact = NUM_BUCKETS // 2
    nf = np.maximum(n, 1).astype(np.float32)
    large = max_exact + (np.log(nf / np.float32(max_exact)) / np.float32(math.log(MAX_DISTANCE / max_exact))
                         * np.float32(NUM_BUCKETS - max_exact)).astype(np.int32)
    large = np.minimum(large, NUM_BUCKETS - 1)
    return np.where(n < max_exact, n, large).astype(np.int32)


def _bias_tables(rel_bias, seq):
    TQ, TK, G = NSA_TQ, NSA_TK, NSA_GROUP
    i = np.arange(TQ)[:, None]
    j = np.arange(TK)[None, :]
    bk = np.stack([_t5_bucket_np(i - j), _t5_bucket_np(TK + i - j),
                   np.full((TQ, TK), NUM_BUCKETS - 1, np.int32)])
    tiles = rel_bias[bk]
    tiles = jnp.transpose(tiles, (3, 0, 1, 2)).reshape(NSA_KV_HEADS, G, 3, TQ, TK)
    tiles = jnp.transpose(tiles, (0, 2, 1, 3, 4)).reshape(NSA_KV_HEADS, 3, G * TQ, TK)
    t = np.arange(seq)[:, None]
    n = np.arange(N_CMP_PAD)[None, :]
    bc = rel_bias[_t5_bucket_np(t - (n * D_CMP + L_CMP - 1))]
    nqt = seq // TQ
    bc = jnp.transpose(bc, (2, 0, 1)).reshape(NSA_KV_HEADS, G, nqt, TQ, N_CMP_PAD)
    bc = jnp.transpose(bc, (0, 2, 1, 3, 4)).reshape(NSA_KV_HEADS, nqt, G * TQ, N_CMP_PAD)
    return tiles.astype(F32), bc.astype(F32)


def _selection_tables(seq):
    n_cmp = (seq - L_CMP) // D_CMP + 1
    n_sel = seq // L_SLC
    cmp_start = np.arange(n_cmp) * D_CMP
    sel_start = np.arange(n_sel) * L_SLC
    overlap = ((cmp_start[:, None] < sel_start[None, :] + L_SLC)
               & (cmp_start[:, None] + L_CMP > sel_start[None, :])).astype(np.float32)
    ovt = np.zeros((LANES, N_CMP_PAD), np.float32)
    ovt[:n_sel, :n_cmp] = overlap.T
    expand = np.zeros((LANES, seq), np.float32)
    expand[np.arange(seq) // L_SLC, np.arange(seq)] = 1.0
    return jnp.asarray(ovt), jnp.asarray(expand, dtype=BF16)


def _permute_w_in(w_in):
    L, D, _ = w_in.shape
    o = np.cumsum([0, 3 * GDN_DIM, GDN_DIM, GDN_HEADS, GDN_HEADS, NSA_DIM, 6 * NSA_KV_HEADS * HEAD_DIM,
                   3 * NSA_HEADS, CONV_DIM, CONV_DIM, CONV_DIM])
    seg = lambda a, b: w_in[:, :, o[a]:o[b]]
    zeros = lambda n: jnp.zeros((L, D, n), w_in.dtype)
    n_small = 2 * GDN_HEADS + 3 * NSA_HEADS
    parts = [seg(0, 2),
             seg(4, 6),
             seg(2, 4), seg(6, 7), zeros(LANES - n_small),
             zeros(LANES),
             seg(7, 10)]
    w = jnp.concatenate(parts, axis=-1)
    assert w.shape[-1] == PROJ_PAD
    return w.astype(BF16)


def kernel(x, rel_bias, norm_mix, w_in, gdn_conv, gdn_a_log, gdn_dt_bias, gdn_norm, nsa_q_norm, nsa_k_norm,
           cmp_pos, cmp_w1, cmp_w2, sconv_w, w_out, norm_ffn, w_gate, w_up, w_down):
    B, S, D = x.shape
    depth = w_in.shape[0]
    T = B * S
    x2 = x.reshape(T, D)

    w_in_p = _permute_w_in(w_in)
    w_out_b = w_out.astype(BF16)
    w_gate_b = w_gate.astype(BF16)
    w_up_b = w_up.astype(BF16)
    w_down_b = w_down.astype(BF16)
    cmp_w1_b = cmp_w1.astype(BF16)
    cmp_w2_b = cmp_w2.astype(BF16)
    conv_t = jnp.transpose(gdn_conv, (0, 2, 1))
    sconv_t = jnp.transpose(sconv_w, (0, 2, 1))
    pad_a = lambda v: jnp.pad(v, ((0, 0), (LANE_A, LANES - LANE_A - GDN_HEADS)))[:, None, :]
    alog_vec = pad_a(gdn_a_log)
    dt_vec = pad_a(gdn_dt_bias)
    bias_tiles, bias_cmp = _bias_tables(rel_bias, S)
    ovt, expand = _selection_tables(S)

    for l in range(depth):
        proj = _inproj(x2, norm_mix[l][None, :], w_in_p[l])
        y_gdn = _gdn(proj, conv_t[l], alog_vec[l], dt_vec[l], gdn_norm[l][None, :], B, S)
        kc, vc = _compress(proj, cmp_pos[l], cmp_w1_b[l], cmp_w2_b[l], nsa_k_norm[l, 0][None, :], B, S)
        y_nsa = _nsa(proj, kc, vc, nsa_q_norm[l][None, :], nsa_k_norm[l, 1:3], bias_tiles, bias_cmp,
                     ovt, expand, B, S)
        x2 = _outproj(x2, y_gdn, y_nsa, proj, sconv_t[l], w_out_b[l], S)
        x2 = _ffn(x2, norm_ffn[l][None, :], w_gate_b[l], w_up_b[l], w_down_b[l])
    return x2.reshape(B, S, D)
```

```python
import functools
import math

import numpy as np
import jax
import jax.numpy as jnp
from jax import lax
from jax.experimental import pallas as pl
from jax.experimental.pallas import tpu as pltpu

HEAD_DIM = 128
GDN_HEADS = 6
NSA_HEADS = 6
NSA_KV_HEADS = 2
NSA_GROUP = NSA_HEADS // NSA_KV_HEADS
CONV_DIM = 512
GDN_DIM = GDN_HEADS * HEAD_DIM
NSA_DIM = NSA_HEADS * HEAD_DIM
GDN_CONV_W = 4
GDN_CHUNK = 64
L_CMP = 32
D_CMP = 16
L_SLC = 64
TOP_N = 8
WINDOW = 512
NUM_BUCKETS = 32
MAX_DISTANCE = 128
SHORT_CONV_W = 3
RMS_EPS = 1e-6
FORCED_SCORE = 1e4
NEG_BIG = -1e30

LANES = 128
SUBLANES = 8
VMEM_LIMIT_BYTES = 56 * 1024 * 1024

BLK_QKV = 0
BLK_Z = 18
BLK_NQ = 24
BLK_NKV = 30
BLK_SMALL = 42
BLK_CU = 44
BLK_CB = 48
BLK_CC = 52
N_BLK = 56
PROJ_PAD = N_BLK * LANES
LANE_BETA = 0
LANE_A = GDN_HEADS
LANE_GATE = 2 * GDN_HEADS

GDN_GROUP_ROWS = 256
GDN_HEADS_PER_PROGRAM = 2
NSA_TQ = 128
NSA_TK = 128
N_CMP_PAD = 128

F32 = jnp.float32
BF16 = jnp.bfloat16


def _dot(a, b):
    return jnp.dot(a, b, preferred_element_type=F32)


def _dot_nt(a, b):
    return lax.dot_general(a, b, (((1,), (1,)), ((), ())), preferred_element_type=F32)


def _dot_tn(a, b):
    return lax.dot_general(a, b, (((0,), (0,)), ((), ())), preferred_element_type=F32)


def _silu(x):
    return x * jax.nn.sigmoid(x)


def _softplus(x):
    return jnp.maximum(x, 0.0) + jnp.log1p(jnp.exp(-jnp.abs(x)))


def _lane_pick(x, lane):
    idx = lax.broadcasted_iota(jnp.int32, x.shape, 1)
    return jnp.sum(jnp.where(idx == lane, x, 0.0), axis=-1, keepdims=True)


def _shift_rows(cur, prev, s):
    rolled = pltpu.roll(cur, s, 0)
    rows = lax.broadcasted_iota(jnp.int32, cur.shape, 0)
    out = rolled
    for r in range(s):
        out = jnp.where(rows == r, prev[SUBLANES - s + r:SUBLANES - s + r + 1, :], out)
    return out


def _inproj_kernel(x_ref, g_ref, w_ref, o_ref, h_ref):
    @pl.when(pl.program_id(1) == 0)
    def _():
        x = x_ref[...]
        ms = jnp.mean(x * x, axis=-1, keepdims=True)
        h_ref[...] = (x * lax.rsqrt(ms + RMS_EPS) * g_ref[...]).astype(BF16)

    o_ref[...] = _dot(h_ref[...], w_ref[...])


def _inproj(x2, g, w, tm=1024, tn=512):
    T, D = x2.shape
    N = w.shape[1]
    return pl.pallas_call(
        _inproj_kernel,
        grid=(T // tm, N // tn),
        in_specs=[
            pl.BlockSpec((tm, D), lambda i, j: (i, 0)),
            pl.BlockSpec((1, D), lambda i, j: (0, 0)),
            pl.BlockSpec((D, tn), lambda i, j: (0, j)),
        ],
        out_specs=pl.BlockSpec((tm, tn), lambda i, j: (i, j)),
        out_shape=jax.ShapeDtypeStruct((T, N), F32),
        scratch_shapes=[pltpu.VMEM((tm, D), BF16)],
        compiler_params=pltpu.CompilerParams(
            dimension_semantics=("parallel", "arbitrary"), vmem_limit_bytes=VMEM_LIMIT_BYTES),
        name="inproj",
    )(x2, g, w)


def _gdn_kernel(q_ref, k_ref, v_ref, z_ref, small_ref, cq_ref, ck_ref, cv_ref, alog_ref, dt_ref,
                ng_ref, o_ref, *, seq):
    R = GDN_GROUP_ROWS
    C = GDN_CHUNK
    n_groups = seq // R
    rows = lax.broadcasted_iota(jnp.int32, (R, R), 0)
    cols = lax.broadcasted_iota(jnp.int32, (R, R), 1)
    same = (rows & -C) == (cols & -C)
    tril = same & (rows >= cols)
    strict = same & (rows > cols)
    row_in_chunk = lax.broadcasted_iota(jnp.int32, (R, LANES), 0) & (C - 1)
    neg_a = -jnp.exp(alog_ref[...])

    def conv_silu(ref, w_ref, ls, r0, first):
        cur = ref[pl.ds(r0, R), ls]
        prev = ref[pl.ds(pl.multiple_of(jnp.maximum(r0 - SUBLANES, 0), SUBLANES), SUBLANES), ls]
        prev = jnp.where(first, 0.0, prev)
        w = w_ref[:, ls]
        y = cur * w[3:4, :]
        for s in range(1, GDN_CONV_W):
            y = y + _shift_rows(cur, prev, s) * w[3 - s:4 - s, :]
        return _silu(y)

    def group(gi, states):
        return tuple(head_group(gi, states[hh], hh) for hh in range(GDN_HEADS_PER_PROGRAM))

    def head_group(gi, state, hh):
        h = pl.program_id(1) * GDN_HEADS_PER_PROGRAM + hh
        ls = slice(hh * HEAD_DIM, (hh + 1) * HEAD_DIM)
        r0 = pl.multiple_of(gi * R, R)
        first = gi == 0
        q = conv_silu(q_ref, cq_ref, ls, r0, first)
        k = conv_silu(k_ref, ck_ref, ls, r0, first)
        v = conv_silu(v_ref, cv_ref, ls, r0, first)
        q = q * lax.rsqrt(jnp.sum(q * q, axis=-1, keepdims=True) + RMS_EPS) * (HEAD_DIM ** -0.5)
        k = k * lax.rsqrt(jnp.sum(k * k, axis=-1, keepdims=True) + RMS_EPS)
        small = small_ref[pl.ds(r0, R), :]
        beta = _lane_pick(jax.nn.sigmoid(small), LANE_BETA + h)
        g = _lane_pick(neg_a * _softplus(small + dt_ref[...]), LANE_A + h)
        gc = jnp.broadcast_to(g, (R, LANES))
        s = 1
        while s < C:
            gc = gc + jnp.where(row_in_chunk >= s, pltpu.roll(gc, s, 0), 0.0)
            s *= 2
        gc_col = jnp.concatenate([gc, gc], axis=1)
        gc_row = gc_col.T
        decay = jnp.where(tril, jnp.exp(jnp.where(tril, gc_col - gc_row, 0.0)), 0.0)
        kb = k * beta
        vb = v * beta
        k16 = k.astype(BF16)
        lmat = jnp.where(strict, _dot_nt(kb.astype(BF16), k16) * decay, 0.0)
        y = -lmat
        p = lmat
        n = 1
        while 2 * n < C:
            p16 = p.astype(BF16)
            p = _dot(p16, p16)
            n *= 2
            y = y + p + _dot(y.astype(BF16), p.astype(BF16))
        eg = jnp.exp(gc)
        rhs = jnp.concatenate([vb, kb * eg], axis=1)
        sol = rhs + _dot(y.astype(BF16), rhs.astype(BF16))
        u = sol[:, :HEAD_DIM]
        w = sol[:, HEAD_DIM:]
        attn = jnp.where(tril, _dot_nt(q.astype(BF16), k16) * decay, 0.0)
        qg = (q * eg).astype(BF16)
        outs = []
        for c in range(R // C):
            sl = slice(c * C, (c + 1) * C)
            g_last = gc[c * C + C - 1:c * C + C, :]
            kdec = k[sl] * jnp.exp(g_last - gc[sl])
            s16 = state.astype(BF16)
            v_new = u[sl] - _dot(w[sl].astype(BF16), s16)
            vn16 = v_new.astype(BF16)
            o_c = _dot(qg[sl], s16) + _dot(attn[sl, sl].astype(BF16), vn16)
            state = state * jnp.exp(g_last[:, 0:1]) + _dot_tn(kdec.astype(BF16), vn16)
            outs.append(o_c)
        o = jnp.concatenate(outs, axis=0)
        o = o * lax.rsqrt(jnp.mean(o * o, axis=-1, keepdims=True) + RMS_EPS) * ng_ref[...]
        o_ref[pl.ds(r0, R), ls] = (o * _silu(z_ref[pl.ds(r0, R), ls])).astype(o_ref.dtype)
        return state

    lax.fori_loop(0, n_groups, group,
                  tuple(jnp.zeros((HEAD_DIM, HEAD_DIM), F32) for _ in range(GDN_HEADS_PER_PROGRAM)))


def _gdn(proj, conv_t, alog_vec, dt_vec, norm_g, batch, seq):
    T = proj.shape[0]
    P = GDN_HEADS_PER_PROGRAM
    W = P * LANES
    col = lambda blk: (lambda b, h: (b, blk // P + h))
    wcol = lambda blk: (lambda b, h: (0, blk // P + h))
    const = lambda b, h: (0, 0)
    return pl.pallas_call(
        functools.partial(_gdn_kernel, seq=seq),
        grid=(batch, GDN_HEADS // P),
        in_specs=[
            pl.BlockSpec((seq, W), col(BLK_QKV)),
            pl.BlockSpec((seq, W), col(BLK_QKV + GDN_HEADS)),
            pl.BlockSpec((seq, W), col(BLK_QKV + 2 * GDN_HEADS)),
            pl.BlockSpec((seq, W), col(BLK_Z)),
            pl.BlockSpec((seq, LANES), lambda b, h: (b, BLK_SMALL)),
            pl.BlockSpec((GDN_CONV_W, W), wcol(0)),
            pl.BlockSpec((GDN_CONV_W, W), wcol(GDN_HEADS)),
            pl.BlockSpec((GDN_CONV_W, W), wcol(2 * GDN_HEADS)),
            pl.BlockSpec((1, LANES), const),
            pl.BlockSpec((1, LANES), const),
            pl.BlockSpec((1, LANES), const),
        ],
        out_specs=pl.BlockSpec((seq, W), lambda b, h: (b, h)),
        out_shape=jax.ShapeDtypeStruct((T, GDN_DIM), BF16),
        compiler_params=pltpu.CompilerParams(
            dimension_semantics=("parallel", "parallel"), vmem_limit_bytes=VMEM_LIMIT_BYTES),
        name="gdn",
    )(proj, proj, proj, proj, proj, conv_t, conv_t, conv_t, alog_vec, dt_vec, norm_g)


def _cmp_kernel(kt_ref, vt_ref, pos_ref, w1_ref, w2_ref, kn_ref, kc_ref, vc_ref, *, seq):
    n_seg = seq // D_CMP
    out_refs = (kc_ref, vc_ref)
    for which, tok_ref in enumerate((kt_ref, vt_ref)):
        hi = jnp.zeros((n_seg, HEAD_DIM), F32)
        lo = jnp.zeros((n_seg, HEAD_DIM), F32)
        for p in range(D_CMP):
            seg = tok_ref[pl.ds(p, n_seg, stride=D_CMP), :]
            a = (seg + pos_ref[which, p:p + 1, :]).astype(BF16)
            hi = hi + _dot(a, w1_ref[which, p * HEAD_DIM:(p + 1) * HEAD_DIM, :])
            b = (seg + pos_ref[which, D_CMP + p:D_CMP + p + 1, :]).astype(BF16)
            lo = lo + _dot(b, w1_ref[which, (D_CMP + p) * HEAD_DIM:(D_CMP + p + 1) * HEAD_DIM, :])
        hid = hi + pltpu.roll(lo, n_seg - 1, 0)
        out = _dot(_silu(hid).astype(BF16), w2_ref[which])
        if which == 0:
            out = out * lax.rsqrt(jnp.mean(out * out, axis=-1, keepdims=True) + RMS_EPS) * kn_ref[...]
        out_refs[which][...] = out.astype(BF16)


def _compress(proj, cmp_pos, w1, w2, kn0, batch, seq):
    n_seg = seq // D_CMP
    shp = jax.ShapeDtypeStruct((batch, NSA_KV_HEADS, n_seg, HEAD_DIM), BF16)
    const3 = lambda b, h: (0, 0, 0)
    return pl.pallas_call(
        functools.partial(_cmp_kernel, seq=seq),
        grid=(batch, NSA_KV_HEADS),
        in_specs=[
            pl.BlockSpec((seq, LANES), lambda b, h: (b, BLK_NKV + h)),
            pl.BlockSpec((seq, LANES), lambda b, h: (b, BLK_NKV + NSA_KV_HEADS + h)),
            pl.BlockSpec((2, L_CMP, HEAD_DIM), const3),
            pl.BlockSpec((2, L_CMP * HEAD_DIM, HEAD_DIM), const3),
            pl.BlockSpec((2, HEAD_DIM, HEAD_DIM), const3),
            pl.BlockSpec((1, HEAD_DIM), lambda b, h: (0, 0)),
        ],
        out_specs=[pl.BlockSpec((None, None, n_seg, HEAD_DIM), lambda b, h: (b, h, 0, 0))] * 2,
        out_shape=[shp, shp],
        compiler_params=pltpu.CompilerParams(
            dimension_semantics=("parallel", "parallel"), vmem_limit_bytes=VMEM_LIMIT_BYTES),
        name="nsa_compress",
    )(proj, proj, cmp_pos, w1, w2, kn0)


def _nsa_kernel(q_ref, kc_ref, vc_ref, ks_ref, vs_ref, kw_ref, vw_ref, small_ref, qn_ref, kn_ref,
                bt_ref, bc_ref, ovt_ref, exp_ref, o_ref,
                ksn_ref, vsb_ref, kwn_ref, vwb_ref, mask_ref, *, seq):
    hkv = pl.program_id(1)
    qt = pl.program_id(2)
    TQ, TK, G = NSA_TQ, NSA_TK, NSA_GROUP
    n_kt = seq // TK
    RQ = G * TQ

    @pl.when(qt == 0)
    def _():
        def norm_rows(i, carry):
            r0 = pl.multiple_of(i * 256, 256)
            for src, dst, gi in ((ks_ref, ksn_ref, 0), (kw_ref, kwn_ref, 1)):
                x = src[pl.ds(r0, 256), :]
                xn = x * lax.rsqrt(jnp.mean(x * x, axis=-1, keepdims=True) + RMS_EPS) * kn_ref[gi:gi + 1, :]
                dst[pl.ds(r0, 256), :] = xn.astype(BF16)
            vsb_ref[pl.ds(r0, 256), :] = vs_ref[pl.ds(r0, 256), :].astype(BF16)
            vwb_ref[pl.ds(r0, 256), :] = vw_ref[pl.ds(r0, 256), :].astype(BF16)
            return carry
        lax.fori_loop(0, seq // 256, norm_rows, 0)

    scale = HEAD_DIM ** -0.5
    qs = []
    for g in range(G):
        x = q_ref[:, g * HEAD_DIM:(g + 1) * HEAD_DIM]
        xn = x * lax.rsqrt(jnp.mean(x * x, axis=-1, keepdims=True) + RMS_EPS) * qn_ref[...]
        qs.append((xn * scale).astype(BF16))
    q = jnp.concatenate(qs, axis=0)

    row = lax.broadcasted_iota(jnp.int32, (RQ, TK), 0) & (TQ - 1)
    col = lax.broadcasted_iota(jnp.int32, (RQ, TK), 1)
    t_row = qt * TQ + row

    s = _dot_nt(q, kc_ref[...]) + bc_ref[...]
    n_cmp = (seq - L_CMP) // D_CMP + 1
    cmp_end = jnp.where(col < n_cmp, col * D_CMP + (L_CMP - 1), seq)
    valid = t_row >= cmp_end
    s = jnp.where(valid, s, NEG_BIG)
    m = jnp.max(s, axis=-1, keepdims=True)
    p = jnp.where(valid, jnp.exp(s - m), 0.0)
    den = jnp.sum(p, axis=-1, keepdims=True)
    p = p / jnp.where(den > 0, den, 1.0)
    o_cmp = _dot(p.astype(BF16), vc_ref[...])

    p_sum = p[0:TQ]
    for g in range(1, G):
        p_sum = p_sum + p[g * TQ:(g + 1) * TQ]
    imp = lax.dot_general(ovt_ref[...], p_sum, (((1,), (1,)), ((), ())),
                          precision=lax.Precision.HIGHEST, preferred_element_type=F32)
    jb = lax.broadcasted_iota(jnp.int32, (LANES, TQ), 0)
    cur = (qt * TQ + lax.broadcasted_iota(jnp.int32, (LANES, TQ), 1)) >> int(math.log2(L_SLC))
    val = jnp.where(jb == 0, FORCED_SCORE, jnp.where(jb >= cur - 1, FORCED_SCORE, imp))
    val = jnp.where(jb > cur, -1.0, val)
    n_sel = seq // L_SLC
    val = jnp.where(jb < n_sel, val, -2.0)
    cnt = jnp.zeros((LANES, TQ), F32)
    for i in range(n_sel):
        r = val[i:i + 1, :]
        cnt = cnt + jnp.where(r > val, 1.0, jnp.where(r == val, jnp.where(jb > i, 1.0, 0.0), 0.0))
    sel = jnp.where(cnt < min(TOP_N, n_sel), 1.0, 0.0).T.astype(BF16)
    for kt in range(n_kt):
        mask_ref[kt] = _dot(sel, exp_ref[:, kt * TK:(kt + 1) * TK])

    def flash_step(carry, k, v, bias, valid):
        m, l, acc = carry
        s = jnp.where(valid, _dot_nt(q, k) + bias, NEG_BIG)
        m_new = jnp.maximum(m, jnp.max(s, axis=-1, keepdims=True))
        alpha = jnp.exp(m - m_new)
        p = jnp.exp(s - m_new)
        l = alpha * l + jnp.sum(p, axis=-1, keepdims=True)
        acc = alpha * acc + _dot(p.astype(BF16), v)
        return m_new, l, acc

    def init():
        return (jnp.full((RQ, 1), NEG_BIG, F32), jnp.zeros((RQ, 1), F32), jnp.zeros((RQ, HEAD_DIM), F32))

    def tile(ref, kt):
        return ref[pl.ds(pl.multiple_of(kt * TK, TK), TK), :]

    def sel_mask(kt):
        mk = mask_ref[kt]
        return jnp.concatenate([mk] * G, axis=0)

    causal = row >= col

    carry = flash_step(init(), tile(ksn_ref, qt), tile(vsb_ref, qt), bt_ref[0],
                       jnp.where(causal, sel_mask(qt), 0.0) > 0.5)
    kt1 = jnp.maximum(qt - 1, 0)
    carry = flash_step(carry, tile(ksn_ref, kt1), tile(vsb_ref, kt1), bt_ref[1],
                       sel_mask(kt1) > jnp.where(qt >= 1, 0.5, 2.0))

    def far_step(kt, carry):
        return flash_step(carry, tile(ksn_ref, kt), tile(vsb_ref, kt), bt_ref[2], sel_mask(kt) > 0.5)

    m, l, acc = lax.fori_loop(0, jnp.maximum(qt - 1, 0), far_step, carry)
    o_slc = acc / l

    n_win = WINDOW // TK
    carry = flash_step(init(), tile(kwn_ref, qt), tile(vwb_ref, qt), bt_ref[0], causal)
    for d in range(1, n_win + 1):
        ktd = jnp.maximum(qt - d, 0)
        if d < n_win:
            inside = col >= jnp.where(qt >= d, 0, TK)
        else:
            inside = (col - row) >= jnp.where(qt >= d, 1, 2 * TK)
        carry = flash_step(carry, tile(kwn_ref, ktd), tile(vwb_ref, ktd), bt_ref[min(d, 2)], inside)
    m, l, acc = carry
    o_win = acc / l

    sig = jax.nn.sigmoid(small_ref[...])
    for g in range(G):
        rs = slice(g * TQ, (g + 1) * TQ)
        lane = LANE_GATE + hkv * G + g
        o = (_lane_pick(sig, lane) * o_cmp[rs]
             + _lane_pick(sig, lane + NSA_HEADS) * o_slc[rs]
             + _lane_pick(sig, lane + 2 * NSA_HEADS) * o_win[rs])
        o_ref[:, g * HEAD_DIM:(g + 1) * HEAD_DIM] = o.astype(o_ref.dtype)


def _nsa(proj, kc, vc, qn, kn12, bias_tiles, bias_cmp, ovt, expand, batch, seq):
    T = proj.shape[0]
    TQ, G = NSA_TQ, NSA_GROUP
    nqt = seq // TQ
    n_kt = seq // NSA_TK
    gw = G * HEAD_DIM
    kvcol = lambda part: (lambda b, h, t: (b, BLK_NKV + part * NSA_KV_HEADS + h))
    return pl.pallas_call(
        functools.partial(_nsa_kernel, seq=seq),
        grid=(batch, NSA_KV_HEADS, nqt),
        in_specs=[
            pl.BlockSpec((TQ, gw), lambda b, h, t: (b * nqt + t, BLK_NQ * LANES // gw + h)),
            pl.BlockSpec((None, None, N_CMP_PAD, HEAD_DIM), lambda b, h, t: (b, h, 0, 0)),
            pl.BlockSpec((None, None, N_CMP_PAD, HEAD_DIM), lambda b, h, t: (b, h, 0, 0)),
            pl.BlockSpec((seq, LANES), kvcol(2)),
            pl.BlockSpec((seq, LANES), kvcol(3)),
            pl.BlockSpec((seq, LANES), kvcol(4)),
            pl.BlockSpec((seq, LANES), kvcol(5)),
            pl.BlockSpec((TQ, LANES), lambda b, h, t: (b * nqt + t, BLK_SMALL)),
            pl.BlockSpec((1, HEAD_DIM), lambda b, h, t: (0, 0)),
            pl.BlockSpec((2, HEAD_DIM), lambda b, h, t: (0, 0)),
            pl.BlockSpec((None, 3, G * TQ, NSA_TK), lambda b, h, t: (h, 0, 0, 0)),
            pl.BlockSpec((None, None, G * TQ, N_CMP_PAD), lambda b, h, t: (h, t, 0, 0)),
            pl.BlockSpec((LANES, N_CMP_PAD), lambda b, h, t: (0, 0)),
            pl.BlockSpec((LANES, seq), lambda b, h, t: (0, 0)),
        ],
        out_specs=pl.BlockSpec((TQ, gw), lambda b, h, t: (b * nqt + t, h)),
        out_shape=jax.ShapeDtypeStruct((T, NSA_DIM), BF16),
        scratch_shapes=[
            pltpu.VMEM((seq, HEAD_DIM), BF16),
            pltpu.VMEM((seq, HEAD_DIM), BF16),
            pltpu.VMEM((seq, HEAD_DIM), BF16),
            pltpu.VMEM((seq, HEAD_DIM), BF16),
            pltpu.VMEM((n_kt, TQ, NSA_TK), F32),
        ],
        compiler_params=pltpu.CompilerParams(
            dimension_semantics=("parallel", "parallel", "arbitrary"), vmem_limit_bytes=VMEM_LIMIT_BYTES),
        name="nsa_attention",
    )(proj, kc, vc, proj, proj, proj, proj, proj, qn, kn12, bias_tiles, bias_cmp, ovt, expand)


def _outproj_kernel(x_ref, yg_ref, yn_ref, cu_ref, cb_ref, cc_ref, cup_ref, ccp_ref, sw_ref,
                    wg_ref, wn_ref, wc_ref, o_ref, *, tiles_per_seq):
    i = pl.program_id(0)
    v = cc_ref[...] * cu_ref[...]
    vp = jnp.where(i % tiles_per_seq == 0, 0.0, ccp_ref[...] * cup_ref[...])
    sw = sw_ref[...]
    y = v * sw[2:3, :]
    for s in range(1, SHORT_CONV_W):
        y = y + _shift_rows(v, vp, s) * sw[2 - s:3 - s, :]
    yc = (cb_ref[...] * y).astype(BF16)
    o_ref[...] = (x_ref[...] + _dot(yg_ref[...], wg_ref[...]) + _dot(yn_ref[...], wn_ref[...])
                  + _dot(yc, wc_ref[...]))


def _outproj(x2, y_gdn, y_nsa, proj, sconv_t, w_out, seq, tm=512):
    T, D = x2.shape
    cb = CONV_DIM // LANES
    prev = lambda blk: (lambda i: (jnp.maximum(i * (tm // SUBLANES) - 1, 0), blk // cb))
    const = lambda i: (0, 0)
    return pl.pallas_call(
        functools.partial(_outproj_kernel, tiles_per_seq=seq // tm),
        grid=(T // tm,),
        in_specs=[
            pl.BlockSpec((tm, D), lambda i: (i, 0)),
            pl.BlockSpec((tm, GDN_DIM), lambda i: (i, 0)),
            pl.BlockSpec((tm, NSA_DIM), lambda i: (i, 0)),
            pl.BlockSpec((tm, CONV_DIM), lambda i: (i, BLK_CU // cb)),
            pl.BlockSpec((tm, CONV_DIM), lambda i: (i, BLK_CB // cb)),
            pl.BlockSpec((tm, CONV_DIM), lambda i: (i, BLK_CC // cb)),
            pl.BlockSpec((SUBLANES, CONV_DIM), prev(BLK_CU)),
            pl.BlockSpec((SUBLANES, CONV_DIM), prev(BLK_CC)),
            pl.BlockSpec((SHORT_CONV_W, CONV_DIM), const),
            pl.BlockSpec((GDN_DIM, D), lambda i: (0, 0)),
            pl.BlockSpec((NSA_DIM, D), lambda i: (GDN_DIM // NSA_DIM, 0)),
            pl.BlockSpec((CONV_DIM, D), lambda i: ((GDN_DIM + NSA_DIM) // CONV_DIM, 0)),
        ],
        out_specs=pl.BlockSpec((tm, D), lambda i: (i, 0)),
        out_shape=jax.ShapeDtypeStruct((T, D), F32),
        compiler_params=pltpu.CompilerParams(
            dimension_semantics=("parallel",), vmem_limit_bytes=VMEM_LIMIT_BYTES),
        name="outproj",
    )(x2, y_gdn, y_nsa, proj, proj, proj, proj, proj, sconv_t, w_out, w_out, w_out)


def _ffn_kernel(x_ref, g_ref, wg_ref, wu_ref, wd_ref, o_ref, h_ref):
    @pl.when(pl.program_id(1) == 0)
    def _():
        x = x_ref[...]
        ms = jnp.mean(x * x, axis=-1, keepdims=True)
        h_ref[...] = (x * lax.rsqrt(ms + RMS_EPS) * g_ref[...]).astype(BF16)
        o_ref[...] = x

    h = h_ref[...]
    a = _dot(h, wg_ref[...])
    b = _dot(h, wu_ref[...])
    o_ref[...] += _dot((_silu(a) * b).astype(BF16), wd_ref[...])


def _ffn(x2, g, w_gate, w_up, w_down, tm=1024, tf=512):
    T, D = x2.shape
    F = w_gate.shape[1]
    return pl.pallas_call(
        _ffn_kernel,
        grid=(T // tm, F // tf),
        in_specs=[
            pl.BlockSpec((tm, D), lambda i, j: (i, 0), pipeline_mode=pl.Buffered(1)),
            pl.BlockSpec((1, D), lambda i, j: (0, 0)),
            pl.BlockSpec((D, tf), lambda i, j: (0, j)),
            pl.BlockSpec((D, tf), lambda i, j: (0, j)),
            pl.BlockSpec((tf, D), lambda i, j: (j, 0)),
        ],
        out_specs=pl.BlockSpec((tm, D), lambda i, j: (i, 0)),
        out_shape=jax.ShapeDtypeStruct((T, D), F32),
        scratch_shapes=[pltpu.VMEM((tm, D), BF16)],
        compiler_params=pltpu.CompilerParams(
            dimension_semantics=("parallel", "arbitrary"), vmem_limit_bytes=VMEM_LIMIT_BYTES),
        name="ffn",
    )(x2, g, w_gate, w_up, w_down)


def _t5_bucket_np(dist):
    n = np.maximum(dist, 0)
    max_exact = NUM_BUCKETS // 2
    nf = np.maximum(n, 1).astype(np.float32)
    large = max_exact + (np.log(nf / np.float32(max_exact)) / np.float32(math.log(MAX_DISTANCE / max_exact))
                         * np.float32(NUM_BUCKETS - max_exact)).astype(np.int32)
    large = np.minimum(large, NUM_BUCKETS - 1)
    return np.where(n < max_exact, n, large).astype(np.int32)


def _bias_tables(rel_bias, seq):
    TQ, TK, G = NSA_TQ, NSA_TK, NSA_GROUP
    i = np.arange(TQ)[:, None]
    j = np.arange(TK)[None, :]
    bk = np.stack([_t5_bucket_np(i - j), _t5_bucket_np(TK + i - j),
                   np.full((TQ, TK), NUM_BUCKETS - 1, np.int32)])
    tiles = rel_bias[bk]
    tiles = jnp.transpose(tiles, (3, 0, 1, 2)).reshape(NSA_KV_HEADS, G, 3, TQ, TK)
    tiles = jnp.transpose(tiles, (0, 2, 1, 3, 4)).reshape(NSA_KV_HEADS, 3, G * TQ, TK)
    t = np.arange(seq)[:, None]
    n = np.arange(N_CMP_PAD)[None, :]
    bc = rel_bias[_t5_bucket_np(t - (n * D_CMP + L_CMP - 1))]
    nqt = seq // TQ
    bc = jnp.transpose(bc, (2, 0, 1)).reshape(NSA_KV_HEADS, G, nqt, TQ, N_CMP_PAD)
    bc = jnp.transpose(bc, (0, 2, 1, 3, 4)).reshape(NSA_KV_HEADS, nqt, G * TQ, N_CMP_PAD)
    return tiles.astype(F32), bc.astype(F32)


def _selection_tables(seq):
    n_cmp = (seq - L_CMP) // D_CMP + 1
    n_sel = seq // L_SLC
    cmp_start = np.arange(n_cmp) * D_CMP
    sel_start = np.arange(n_sel) * L_SLC
    overlap = ((cmp_start[:, None] < sel_start[None, :] + L_SLC)
               & (cmp_start[:, None] + L_CMP > sel_start[None, :])).astype(np.float32)
    ovt = np.zeros((LANES, N_CMP_PAD), np.float32)
    ovt[:n_sel, :n_cmp] = overlap.T
    expand = np.zeros((LANES, seq), np.float32)
    expand[np.arange(seq) // L_SLC, np.arange(seq)] = 1.0
    return jnp.asarray(ovt), jnp.asarray(expand, dtype=BF16)


def _permute_w_in(w_in):
    L, D, _ = w_in.shape
    o = np.cumsum([0, 3 * GDN_DIM, GDN_DIM, GDN_HEADS, GDN_HEADS, NSA_DIM, 6 * NSA_KV_HEADS * HEAD_DIM,
                   3 * NSA_HEADS, CONV_DIM, CONV_DIM, CONV_DIM])
    seg = lambda a, b: w_in[:, :, o[a]:o[b]]
    zeros = lambda n: jnp.zeros((L, D, n), w_in.dtype)
    n_small = 2 * GDN_HEADS + 3 * NSA_HEADS
    parts = [seg(0, 2),
             seg(4, 6),
             seg(2, 4), seg(6, 7), zeros(LANES - n_small),
             zeros(LANES),
             seg(7, 10)]
    w = jnp.concatenate(parts, axis=-1)
    assert w.shape[-1] == PROJ_PAD
    return w.astype(BF16)


def kernel(x, rel_bias, norm_mix, w_in, gdn_conv, gdn_a_log, gdn_dt_bias, gdn_norm, nsa_q_norm, nsa_k_norm,
           cmp_pos, cmp_w1, cmp_w2, sconv_w, w_out, norm_ffn, w_gate, w_up, w_down):
    B, S, D = x.shape
    depth = w_in.shape[0]
    T = B * S
    x2 = x.reshape(T, D)

    w_in_p = _permute_w_in(w_in)
    w_out_b = w_out.astype(BF16)
    w_gate_b = w_gate.astype(BF16)
    w_up_b = w_up.astype(BF16)
    w_down_b = w_down.astype(BF16)
    cmp_w1_b = cmp_w1.astype(BF16)
    cmp_w2_b = cmp_w2.astype(BF16)
    conv_t = jnp.transpose(gdn_conv, (0, 2, 1))
    sconv_t = jnp.transpose(sconv_w, (0, 2, 1))
    pad_a = lambda v: jnp.pad(v, ((0, 0), (LANE_A, LANES - LANE_A - GDN_HEADS)))[:, None, :]
    alog_vec = pad_a(gdn_a_log)
    dt_vec = pad_a(gdn_dt_bias)
    bias_tiles, bias_cmp = _bias_tables(rel_bias, S)
    ovt, expand = _selection_tables(S)

    for l in range(depth):
        proj = _inproj(x2, norm_mix[l][None, :], w_in_p[l])
        y_gdn = _gdn(proj, conv_t[l], alog_vec[l], dt_vec[l], gdn_norm[l][None, :], B, S)
        kc, vc = _compress(proj, cmp_pos[l], cmp_w1_b[l], cmp_w2_b[l], nsa_k_norm[l, 0][None, :], B, S)
        y_nsa = _nsa(proj, kc, vc, nsa_q_norm[l][None, :], nsa_k_norm[l, 1:3], bias_tiles, bias_cmp,
                     ovt, expand, B, S)
        x2 = _outproj(x2, y_gdn, y_nsa, proj, sconv_t[l], w_out_b[l], S)
        x2 = _ffn(x2, norm_ffn[l][None, :], w_gate_b[l], w_up_b[l], w_down_b[l])
    return x2.reshape(B, S, D)
```

```python
import functools
import math

import numpy as np
import jax
import jax.numpy as jnp
from jax import lax
from jax.experimental import pallas as pl
from jax.experimental.pallas import tpu as pltpu

HEAD_DIM = 128
GDN_HEADS = 6
NSA_HEADS = 6
NSA_KV_HEADS = 2
NSA_GROUP = NSA_HEADS // NSA_KV_HEADS
CONV_DIM = 512
GDN_DIM = GDN_HEADS * HEAD_DIM
NSA_DIM = NSA_HEADS * HEAD_DIM
GDN_CONV_W = 4
GDN_CHUNK = 64
L_CMP = 32
D_CMP = 16
L_SLC = 64
TOP_N = 8
WINDOW = 512
NUM_BUCKETS = 32
MAX_DISTANCE = 128
SHORT_CONV_W = 3
RMS_EPS = 1e-6
FORCED_SCORE = 1e4
NEG_BIG = -1e30
LOG2E = math.log2(math.e)

LANES = 128
SUBLANES = 8
VMEM_LIMIT_BYTES = 56 * 1024 * 1024

BLK_QKV = 0
BLK_Z = 18
BLK_NQ = 24
BLK_NKV = 30
BLK_SMALL = 42
BLK_CU = 44
BLK_CB = 48
BLK_CC = 52
N_BLK = 56
PROJ_PAD = N_BLK * LANES
LANE_BETA = 0
LANE_A = GDN_HEADS
LANE_GATE = 2 * GDN_HEADS

GDN_GROUP_ROWS = 256
GDN_HEADS_PER_PROGRAM = 3
N_CMP_PAD = 128
NSA_TQ = 256
NSA_TK = 256

F32 = jnp.float32
BF16 = jnp.bfloat16


def _band_zero(seq):
    return (seq // NSA_TQ - 1) * (NSA_TQ // D_CMP)


def _dot(a, b):
    return jnp.dot(a, b, preferred_element_type=F32)


def _dot_nt(a, b):
    return lax.dot_general(a, b, (((1,), (1,)), ((), ())), preferred_element_type=F32)


def _dot_tn(a, b):
    return lax.dot_general(a, b, (((0,), (0,)), ((), ())), preferred_element_type=F32)


def _silu(x):
    return x * jax.nn.sigmoid(x)


def _softplus(x):
    return jnp.maximum(x, 0.0) + jnp.log1p(jnp.exp(-jnp.abs(x)))


def _lane_pick(x, lane):
    idx = lax.broadcasted_iota(jnp.int32, x.shape, 1)
    return jnp.sum(jnp.where(idx == lane, x, 0.0), axis=-1, keepdims=True)


def _round_robin(gens):
    results = [None] * len(gens)
    live = list(range(len(gens)))
    while live:
        for i in list(live):
            try:
                next(gens[i])
            except StopIteration as stop:
                results[i] = stop.value
                live.remove(i)
    return tuple(results)


def _shift_rows(cur, prev, s):
    rolled = pltpu.roll(cur, s, 0)
    rows = lax.broadcasted_iota(jnp.int32, cur.shape, 0)
    out = rolled
    for r in range(s):
        out = jnp.where(rows == r, prev[SUBLANES - s + r:SUBLANES - s + r + 1, :], out)
    return out


def _inproj_kernel(x_ref, g_ref, w_ref, o_ref, h_ref):
    @pl.when(pl.program_id(1) == 0)
    def _():
        x = x_ref[...]
        ms = jnp.mean(x * x, axis=-1, keepdims=True)
        h_ref[...] = (x * lax.rsqrt(ms + RMS_EPS) * g_ref[...]).astype(BF16)

    o_ref[...] = _dot(h_ref[...], w_ref[...])


def _inproj(x2, g, w, tm=1024, tn=512):
    T, D = x2.shape
    N = w.shape[1]
    return pl.pallas_call(
        _inproj_kernel,
        grid=(T // tm, N // tn),
        in_specs=[
            pl.BlockSpec((tm, D), lambda i, j: (i, 0)),
            pl.BlockSpec((1, D), lambda i, j: (0, 0)),
            pl.BlockSpec((D, tn), lambda i, j: (0, j)),
        ],
        out_specs=pl.BlockSpec((tm, tn), lambda i, j: (i, j)),
        out_shape=jax.ShapeDtypeStruct((T, N), F32),
        scratch_shapes=[pltpu.VMEM((tm, D), BF16)],
        compiler_params=pltpu.CompilerParams(
            dimension_semantics=("parallel", "arbitrary"), vmem_limit_bytes=VMEM_LIMIT_BYTES),
        name="inproj",
    )(x2, g, w)


def _gdn_kernel(q_ref, k_ref, v_ref, z_ref, small_ref, cq_ref, ck_ref, cv_ref, alog_ref, dt_ref,
                ng_ref, o_ref, *, seq):
    R = GDN_GROUP_ROWS
    C = GDN_CHUNK
    n_groups = seq // R
    rows = lax.broadcasted_iota(jnp.int32, (R, R), 0)
    cols = lax.broadcasted_iota(jnp.int32, (R, R), 1)
    same = (rows & -C) == (cols & -C)
    tril = same & (rows >= cols)
    strict = same & (rows > cols)
    row_in_chunk = lax.broadcasted_iota(jnp.int32, (R, LANES), 0) & (C - 1)
    neg_a = -jnp.exp(alog_ref[...])

    def conv_silu(ref, w_ref, ls, r0, first):
        cur = ref[pl.ds(r0, R), ls]
        prev = ref[pl.ds(pl.multiple_of(jnp.maximum(r0 - SUBLANES, 0), SUBLANES), SUBLANES), ls]
        prev = jnp.where(first, 0.0, prev)
        w = w_ref[:, ls]
        y = cur * w[3:4, :]
        for s in range(1, GDN_CONV_W):
            y = y + _shift_rows(cur, prev, s) * w[3 - s:4 - s, :]
        return _silu(y)

    def group(gi, states):
        return _round_robin([head_group(gi, states[hh], hh) for hh in range(GDN_HEADS_PER_PROGRAM)])

    def head_group(gi, state, hh):
        h = pl.program_id(1) * GDN_HEADS_PER_PROGRAM + hh
        ls = slice(hh * HEAD_DIM, (hh + 1) * HEAD_DIM)
        r0 = pl.multiple_of(gi * R, R)
        first = gi == 0
        q = conv_silu(q_ref, cq_ref, ls, r0, first)
        k = conv_silu(k_ref, ck_ref, ls, r0, first)
        v = conv_silu(v_ref, cv_ref, ls, r0, first)
        q = q * lax.rsqrt(jnp.sum(q * q, axis=-1, keepdims=True) + RMS_EPS) * (HEAD_DIM ** -0.5)
        k = k * lax.rsqrt(jnp.sum(k * k, axis=-1, keepdims=True) + RMS_EPS)
        small = small_ref[pl.ds(r0, R), :]
        beta = _lane_pick(jax.nn.sigmoid(small), LANE_BETA + h)
        g = _lane_pick(neg_a * _softplus(small + dt_ref[...]), LANE_A + h)
        gc = jnp.broadcast_to(g, (R, LANES))
        s = 1
        while s < C:
            gc = gc + jnp.where(row_in_chunk >= s, pltpu.roll(gc, s, 0), 0.0)
            s *= 2
        gc_col = jnp.concatenate([gc] * (R // LANES), axis=1)
        gc_row = gc_col.T
        decay = jnp.where(tril, jnp.exp(jnp.where(tril, gc_col - gc_row, 0.0)), 0.0)
        kb = k * beta
        vb = v * beta
        k16 = k.astype(BF16)
        yield
        lmat = jnp.where(strict, _dot_nt(kb.astype(BF16), k16) * decay, 0.0)
        attn = jnp.where(tril, _dot_nt(q.astype(BF16), k16) * decay, 0.0)
        yield
        y = -lmat
        p = lmat
        n = 1
        while 2 * n < C:
            p16 = p.astype(BF16)
            p = _dot(p16, p16)
            yield
            n *= 2
            y = y + p + _dot(y.astype(BF16), p.astype(BF16))
            yield
        eg = jnp.exp(gc)
        rhs = jnp.concatenate([vb, kb * eg], axis=1)
        sol = rhs + _dot(y.astype(BF16), rhs.astype(BF16))
        yield
        u = sol[:, :HEAD_DIM]
        w = sol[:, HEAD_DIM:]
        qg = (q * eg).astype(BF16)
        outs = []
        for c in range(R // C):
            sl = slice(c * C, (c + 1) * C)
            g_last = gc[c * C + C - 1:c * C + C, :]
            kdec = k[sl] * jnp.exp(g_last - gc[sl])
            s16 = state.astype(BF16)
            v_new = u[sl] - _dot(w[sl].astype(BF16), s16)
            yield
            vn16 = v_new.astype(BF16)
            o_c = _dot(qg[sl], s16) + _dot(attn[sl, sl].astype(BF16), vn16)
            state = state * jnp.exp(g_last[:, 0:1]) + _dot_tn(kdec.astype(BF16), vn16)
            yield
            outs.append(o_c)
        o = jnp.concatenate(outs, axis=0)
        o = o * lax.rsqrt(jnp.mean(o * o, axis=-1, keepdims=True) + RMS_EPS) * ng_ref[...]
        o_ref[pl.ds(r0, R), ls] = (o * _silu(z_ref[pl.ds(r0, R), ls])).astype(o_ref.dtype)
        return state

    lax.fori_loop(0, n_groups, group,
                  tuple(jnp.zeros((HEAD_DIM, HEAD_DIM), F32) for _ in range(GDN_HEADS_PER_PROGRAM)))


def _gdn(proj, conv_t, alog_vec, dt_vec, norm_g, batch, seq):
    T = proj.shape[0]
    P = GDN_HEADS_PER_PROGRAM
    W = P * LANES
    col = lambda blk: (lambda b, h: (b, blk // P + h))
    wcol = lambda blk: (lambda b, h: (0, blk // P + h))
    const = lambda b, h: (0, 0)
    return pl.pallas_call(
        functools.partial(_gdn_kernel, seq=seq),
        grid=(batch, GDN_HEADS // P),
        in_specs=[
            pl.BlockSpec((seq, W), col(BLK_QKV)),
            pl.BlockSpec((seq, W), col(BLK_QKV + GDN_HEADS)),
            pl.BlockSpec((seq, W), col(BLK_QKV + 2 * GDN_HEADS)),
            pl.BlockSpec((seq, W), col(BLK_Z)),
            pl.BlockSpec((seq, LANES), lambda b, h: (b, BLK_SMALL)),
            pl.BlockSpec((GDN_CONV_W, W), wcol(0)),
            pl.BlockSpec((GDN_CONV_W, W), wcol(GDN_HEADS)),
            pl.BlockSpec((GDN_CONV_W, W), wcol(2 * GDN_HEADS)),
            pl.BlockSpec((1, LANES), const),
            pl.BlockSpec((1, LANES), const),
            pl.BlockSpec((1, LANES), const),
        ],
        out_specs=pl.BlockSpec((seq, W), lambda b, h: (b, h)),
        out_shape=jax.ShapeDtypeStruct((T, GDN_DIM), BF16),
        compiler_params=pltpu.CompilerParams(
            dimension_semantics=("parallel", "parallel"), vmem_limit_bytes=VMEM_LIMIT_BYTES),
        name="gdn",
    )(proj, proj, proj, proj, proj, conv_t, conv_t, conv_t, alog_vec, dt_vec, norm_g)


def _cmp_kernel(kt_ref, vt_ref, pos_ref, w1_ref, w2_ref, kn_ref, kc_ref, vc_ref, *, seq):
    n_seg = seq // D_CMP
    out_refs = (kc_ref, vc_ref)
    for which, tok_ref in enumerate((kt_ref, vt_ref)):
        hi = jnp.zeros((n_seg, HEAD_DIM), F32)
        lo = jnp.zeros((n_seg, HEAD_DIM), F32)
        for p in range(D_CMP):
            seg = tok_ref[pl.ds(p, n_seg, stride=D_CMP), :]
            a = (seg + pos_ref[which, p:p + 1, :]).astype(BF16)
            hi = hi + _dot(a, w1_ref[which, p * HEAD_DIM:(p + 1) * HEAD_DIM, :])
            b = (seg + pos_ref[which, D_CMP + p:D_CMP + p + 1, :]).astype(BF16)
            lo = lo + _dot(b, w1_ref[which, (D_CMP + p) * HEAD_DIM:(D_CMP + p + 1) * HEAD_DIM, :])
        hid = hi + pltpu.roll(lo, n_seg - 1, 0)
        out = _dot(_silu(hid).astype(BF16), w2_ref[which])
        if which == 0:
            out = out * lax.rsqrt(jnp.mean(out * out, axis=-1, keepdims=True) + RMS_EPS) * kn_ref[...]
        out_refs[which][...] = out.astype(BF16)


def _compress(proj, cmp_pos, w1, w2, kn0, batch, seq):
    n_seg = seq // D_CMP
    shp = jax.ShapeDtypeStruct((batch, NSA_KV_HEADS, n_seg, HEAD_DIM), BF16)
    const3 = lambda b, h: (0, 0, 0)
    return pl.pallas_call(
        functools.partial(_cmp_kernel, seq=seq),
        grid=(batch, NSA_KV_HEADS),
        in_specs=[
            pl.BlockSpec((seq, LANES), lambda b, h: (b, BLK_NKV + h)),
            pl.BlockSpec((seq, LANES), lambda b, h: (b, BLK_NKV + NSA_KV_HEADS + h)),
            pl.BlockSpec((2, L_CMP, HEAD_DIM), const3),
            pl.BlockSpec((2, L_CMP * HEAD_DIM, HEAD_DIM), const3),
            pl.BlockSpec((2, HEAD_DIM, HEAD_DIM), const3),
            pl.BlockSpec((1, HEAD_DIM), lambda b, h: (0, 0)),
        ],
        out_specs=[pl.BlockSpec((None, None, n_seg, HEAD_DIM), lambda b, h: (b, h, 0, 0))] * 2,
        out_shape=[shp, shp],
        compiler_params=pltpu.CompilerParams(
            dimension_semantics=("parallel", "parallel"), vmem_limit_bytes=VMEM_LIMIT_BYTES),
        name="nsa_compress",
    )(proj, proj, cmp_pos, w1, w2, kn0)


def _nsa_kernel(q_ref, kc_ref, vc_ref, ks_ref, vs_ref, kw_ref, vw_ref, small_ref, qn_ref, kn_ref,
                bt_ref, cv_ref, band_ref, ovt_ref, o_ref,
                ksn_ref, vst_ref, kwn_ref, vwt_ref, am_ref, st_ref, vct_ref, *, seq):
    hkv = pl.program_id(1)
    qt = pl.program_id(2)
    TQ, TK, G = NSA_TQ, NSA_TK, NSA_GROUP
    LQ = G * TQ
    n_kt = seq // TK
    blocks_per_tile = TK // L_SLC

    @pl.when(qt == 0)
    def _():
        for kt in range(n_kt):
            rs = slice(kt * TK, (kt + 1) * TK)
            for src, dst, gi in ((ks_ref, ksn_ref, 0), (kw_ref, kwn_ref, 1)):
                x = src[rs, :]
                xn = x * lax.rsqrt(jnp.mean(x * x, axis=-1, keepdims=True) + RMS_EPS) * kn_ref[gi:gi + 1, :]
                dst[rs, :] = xn.astype(BF16)
            vst_ref[kt] = vs_ref[rs, :].T.astype(BF16)
            vwt_ref[kt] = vw_ref[rs, :].T.astype(BF16)
        vct_ref[...] = vc_ref[...].astype(F32).T.astype(BF16)

    scale = (HEAD_DIM ** -0.5) * LOG2E
    qs = []
    for g in range(G):
        x = q_ref[:, g * HEAD_DIM:(g + 1) * HEAD_DIM]
        xn = x * lax.rsqrt(jnp.mean(x * x, axis=-1, keepdims=True) + RMS_EPS) * qn_ref[...]
        qs.append((xn * scale).astype(BF16))
    q = jnp.concatenate(qs, axis=0)

    krow = lax.broadcasted_iota(jnp.int32, (TK, LQ), 0)
    qlane = lax.broadcasted_iota(jnp.int32, (TK, LQ), 1) & (TQ - 1)
    far_bias = cv_ref[...]

    def scores(k_ref, kt):
        return _dot_nt(k_ref[pl.ds(pl.multiple_of(kt * TK, TK), TK), :], q)

    def sel_rows(kt, extra):
        parts = []
        for b in range(blocks_per_tile):
            r = am_ref[pl.ds(kt * blocks_per_tile + b, 1), :] + extra
            parts.append(jnp.broadcast_to(r, (L_SLC, LQ)))
        return jnp.concatenate(parts, axis=0)

    def tile(carry, k_ref, vt_ref, kt, finish):
        s = scores(k_ref, kt)
        yield
        s = finish(s)
        m, l, acc = carry
        m_new = jnp.maximum(m, jnp.max(s, axis=0, keepdims=True))
        alpha = jnp.exp2(m - m_new)
        p = jnp.exp2(s - m_new)
        l = alpha * l + jnp.sum(p, axis=0, keepdims=True)
        pv = _dot(vt_ref[kt], p.astype(BF16))
        yield
        return m_new, l, alpha * acc + pv

    def init():
        return (jnp.full((1, LQ), NEG_BIG, F32), jnp.zeros((1, LQ), F32), jnp.zeros((HEAD_DIM, LQ), F32))

    def cmp_select_slc0():
        n_cmp = (seq - L_CMP) // D_CMP + 1
        band0 = pl.multiple_of(_band_zero(seq) - qt * (TQ // D_CMP), TQ // D_CMP)
        s = _dot_nt(kc_ref[...], q) + band_ref[pl.ds(band0, N_CMP_PAD), :]
        yield
        nrow = lax.broadcasted_iota(jnp.int32, (N_CMP_PAD, LQ), 0)
        t_lane = qt * TQ + (lax.broadcasted_iota(jnp.int32, (N_CMP_PAD, LQ), 1) & (TQ - 1))
        cmp_end = jnp.where(nrow < n_cmp, nrow * D_CMP + (L_CMP - 1), seq)
        valid = t_lane >= cmp_end
        s = jnp.where(valid, s, NEG_BIG)
        m = jnp.max(s, axis=0, keepdims=True)
        p = jnp.where(valid, jnp.exp2(s - m), 0.0)
        den = jnp.sum(p, axis=0, keepdims=True)
        p = p / jnp.where(den > 0, den, 1.0)
        o_cmp = _dot(vct_ref[...], p.astype(BF16))
        p_sum = p[:, 0:TQ]
        for g in range(1, G):
            p_sum = p_sum + p[:, g * TQ:(g + 1) * TQ]
        n_sel = seq // L_SLC
        imp = jnp.dot(ovt_ref[...], p_sum, precision=lax.Precision.HIGHEST,
                      preferred_element_type=F32)
        yield
        jb = lax.broadcasted_iota(jnp.int32, (n_sel, TQ), 0)
        cur = (qt * TQ + lax.broadcasted_iota(jnp.int32, (n_sel, TQ), 1)) >> int(math.log2(L_SLC))
        val = jnp.where(jb == 0, FORCED_SCORE, jnp.where(jb >= cur - 1, FORCED_SCORE, imp))
        val = jnp.where(jb > cur, -1.0, val)
        cnt = jnp.zeros((n_sel, TQ), F32)
        for i in range(n_sel):
            r = val[i:i + 1, :]
            cnt = cnt + jnp.where(r > val, 1.0, jnp.where(r == val, jnp.where(jb > i, 1.0, 0.0), 0.0))
        add_mask = jnp.where(cnt < min(TOP_N, n_sel), 0.0, NEG_BIG)
        am_ref[...] = jnp.concatenate([add_mask] * G, axis=1)
        diag = lambda s: jnp.where(krow <= qlane, s + bt_ref[0] + sel_rows(qt, 0.0), NEG_BIG)
        carry = yield from tile(init(), ksn_ref, vst_ref, qt, diag)
        return o_cmp, carry

    def win0():
        return (yield from tile(init(), kwn_ref, vwt_ref, qt,
                                lambda s: jnp.where(krow <= qlane, s + bt_ref[0], NEG_BIG)))

    (o_cmp, c_slc), c_win = _round_robin([cmp_select_slc0(), win0()])

    def prev_tiles(carries):
        c_slc, c_win = carries
        kt = qt - 1
        return _round_robin([
            tile(c_slc, ksn_ref, vst_ref, kt, lambda s: s + bt_ref[1] + sel_rows(kt, 0.0)),
            tile(c_win, kwn_ref, vwt_ref, kt, lambda s: s + bt_ref[1])])

    def edge_tiles(carries):
        c_slc, c_win = carries
        return _round_robin([
            tile(c_slc, ksn_ref, vst_ref, 0, lambda s: s + sel_rows(0, far_bias)),
            tile(c_win, kwn_ref, vwt_ref, qt - 2, lambda s: jnp.where(qlane < krow, s + far_bias, NEG_BIG))])

    carries = lax.cond(qt >= 1, prev_tiles, lambda c: c, (c_slc, c_win))
    c_slc, c_win = lax.cond(qt >= 2, edge_tiles, lambda c: c, carries)

    def slc_far(kt, carry):
        return _round_robin([tile(carry, ksn_ref, vst_ref, kt, lambda s: s + sel_rows(kt, far_bias))])[0]

    m, l, acc = lax.fori_loop(1, jnp.maximum(qt - 1, 1), slc_far, c_slc)
    o_slc = acc / l
    m, l, acc = c_win
    o_win = acc / l

    st_ref[...] = small_ref[...].T

    def gate_row(branch):
        rows = [st_ref[pl.ds(LANE_GATE + branch * NSA_HEADS + hkv * G + g, 1), :] for g in range(G)]
        return jax.nn.sigmoid(jnp.concatenate(rows, axis=1))

    o = gate_row(0) * o_cmp + gate_row(1) * o_slc + gate_row(2) * o_win
    for g in range(G):
        o_ref[:, g * HEAD_DIM:(g + 1) * HEAD_DIM] = o[:, g * TQ:(g + 1) * TQ].T.astype(o_ref.dtype)


def _nsa(proj, kc, vc, qn, kn12, bias_tiles, far_vec, band, ovt, batch, seq):
    T = proj.shape[0]
    TQ, TK, G = NSA_TQ, NSA_TK, NSA_GROUP
    nqt = seq // TQ
    n_kt = seq // TK
    gw = G * HEAD_DIM
    LQ = G * TQ
    n_sel = seq // L_SLC
    kvcol = lambda part: (lambda b, h, t: (b, BLK_NKV + part * NSA_KV_HEADS + h))
    return pl.pallas_call(
        functools.partial(_nsa_kernel, seq=seq),
        grid=(batch, NSA_KV_HEADS, nqt),
        in_specs=[
            pl.BlockSpec((TQ, gw), lambda b, h, t: (b * nqt + t, BLK_NQ * LANES // gw + h)),
            pl.BlockSpec((None, None, N_CMP_PAD, HEAD_DIM), lambda b, h, t: (b, h, 0, 0)),
            pl.BlockSpec((None, None, N_CMP_PAD, HEAD_DIM), lambda b, h, t: (b, h, 0, 0)),
            pl.BlockSpec((seq, LANES), kvcol(2)),
            pl.BlockSpec((seq, LANES), kvcol(3)),
            pl.BlockSpec((seq, LANES), kvcol(4)),
            pl.BlockSpec((seq, LANES), kvcol(5)),
            pl.BlockSpec((TQ, LANES), lambda b, h, t: (b * nqt + t, BLK_SMALL)),
            pl.BlockSpec((1, HEAD_DIM), lambda b, h, t: (0, 0)),
            pl.BlockSpec((2, HEAD_DIM), lambda b, h, t: (0, 0)),
            pl.BlockSpec((None, 2, TK, LQ), lambda b, h, t: (h, 0, 0, 0)),
            pl.BlockSpec((None, 1, LQ), lambda b, h, t: (h, 0, 0)),
            pl.BlockSpec((None, _band_zero(seq) + N_CMP_PAD, LQ), lambda b, h, t: (h, 0, 0)),
            pl.BlockSpec((n_sel, N_CMP_PAD), lambda b, h, t: (0, 0)),
        ],
        out_specs=pl.BlockSpec((TQ, gw), lambda b, h, t: (b * nqt + t, h)),
        out_shape=jax.ShapeDtypeStruct((T, NSA_DIM), BF16),
        scratch_shapes=[
            pltpu.VMEM((seq, HEAD_DIM), BF16),
            pltpu.VMEM((n_kt, HEAD_DIM, TK), BF16),
            pltpu.VMEM((seq, HEAD_DIM), BF16),
            pltpu.VMEM((n_kt, HEAD_DIM, TK), BF16),
            pltpu.VMEM((n_sel, LQ), F32),
            pltpu.VMEM((LANES, TQ), F32),
            pltpu.VMEM((HEAD_DIM, N_CMP_PAD), BF16),
        ],
        compiler_params=pltpu.CompilerParams(
            dimension_semantics=("parallel", "parallel", "arbitrary"), vmem_limit_bytes=VMEM_LIMIT_BYTES),
        name="nsa_attention",
    )(proj, kc, vc, proj, proj, proj, proj, proj, qn, kn12, bias_tiles, far_vec, band, ovt)


def _outproj_kernel(x_ref, yg_ref, yn_ref, cu_ref, cb_ref, cc_ref, cup_ref, ccp_ref, sw_ref,
                    wg_ref, wn_ref, wc_ref, o_ref, *, tiles_per_seq):
    i = pl.program_id(0)
    v = cc_ref[...] * cu_ref[...]
    vp = jnp.where(i % tiles_per_seq == 0, 0.0, ccp_ref[...] * cup_ref[...])
    sw = sw_ref[...]
    y = v * sw[2:3, :]
    for s in range(1, SHORT_CONV_W):
        y = y + _shift_rows(v, vp, s) * sw[2 - s:3 - s, :]
    yc = (cb_ref[...] * y).astype(BF16)
    o_ref[...] = (x_ref[...] + _dot(yg_ref[...], wg_ref[...]) + _dot(yn_ref[...], wn_ref[...])
                  + _dot(yc, wc_ref[...]))


def _outproj(x2, y_gdn, y_nsa, proj, sconv_t, w_out, seq, tm=512):
    T, D = x2.shape
    cb = CONV_DIM // LANES
    prev = lambda blk: (lambda i: (jnp.maximum(i * (tm // SUBLANES) - 1, 0), blk // cb))
    const = lambda i: (0, 0)
    return pl.pallas_call(
        functools.partial(_outproj_kernel, tiles_per_seq=seq // tm),
        grid=(T // tm,),
        in_specs=[
            pl.BlockSpec((tm, D), lambda i: (i, 0)),
            pl.BlockSpec((tm, GDN_DIM), lambda i: (i, 0)),
            pl.BlockSpec((tm, NSA_DIM), lambda i: (i, 0)),
            pl.BlockSpec((tm, CONV_DIM), lambda i: (i, BLK_CU // cb)),
            pl.BlockSpec((tm, CONV_DIM), lambda i: (i, BLK_CB // cb)),
            pl.BlockSpec((tm, CONV_DIM), lambda i: (i, BLK_CC // cb)),
            pl.BlockSpec((SUBLANES, CONV_DIM), prev(BLK_CU)),
            pl.BlockSpec((SUBLANES, CONV_DIM), prev(BLK_CC)),
            pl.BlockSpec((SHORT_CONV_W, CONV_DIM), const),
            pl.BlockSpec((GDN_DIM, D), lambda i: (0, 0)),
            pl.BlockSpec((NSA_DIM, D), lambda i: (GDN_DIM // NSA_DIM, 0)),
            pl.BlockSpec((CONV_DIM, D), lambda i: ((GDN_DIM + NSA_DIM) // CONV_DIM, 0)),
        ],
        out_specs=pl.BlockSpec((tm, D), lambda i: (i, 0)),
        out_shape=jax.ShapeDtypeStruct((T, D), F32),
        compiler_params=pltpu.CompilerParams(
            dimension_semantics=("parallel",), vmem_limit_bytes=VMEM_LIMIT_BYTES),
        name="outproj",
    )(x2, y_gdn, y_nsa, proj, proj, proj, proj, proj, sconv_t, w_out, w_out, w_out)


def _ffn_kernel(x_ref, g_ref, wg_ref, wu_ref, wd_ref, o_ref, h_ref):
    @pl.when(pl.program_id(1) == 0)
    def _():
        x = x_ref[...]
        ms = jnp.mean(x * x, axis=-1, keepdims=True)
        h_ref[...] = (x * lax.rsqrt(ms + RMS_EPS) * g_ref[...]).astype(BF16)
        o_ref[...] = x

    h = h_ref[...]
    a = _dot(h, wg_ref[...])
    b = _dot(h, wu_ref[...])
    o_ref[...] += _dot((_silu(a) * b).astype(BF16), wd_ref[...])


def _ffn(x2, g, w_gate, w_up, w_down, tm=1024, tf=512):
    T, D = x2.shape
    F = w_gate.shape[1]
    return pl.pallas_call(
        _ffn_kernel,
        grid=(T // tm, F // tf),
        in_specs=[
            pl.BlockSpec((tm, D), lambda i, j: (i, 0), pipeline_mode=pl.Buffered(1)),
            pl.BlockSpec((1, D), lambda i, j: (0, 0)),
            pl.BlockSpec((D, tf), lambda i, j: (0, j)),
            pl.BlockSpec((D, tf), lambda i, j: (0, j)),
            pl.BlockSpec((tf, D), lambda i, j: (j, 0)),
        ],
        out_specs=pl.BlockSpec((tm, D), lambda i, j: (i, 0)),
        out_shape=jax.ShapeDtypeStruct((T, D), F32),
        scratch_shapes=[pltpu.VMEM((tm, D), BF16)],
        compiler_params=pltpu.CompilerParams(
            dimension_semantics=("parallel", "arbitrary"), vmem_limit_bytes=VMEM_LIMIT_BYTES),
        name="ffn",
    )(x2, g, w_gate, w_up, w_down)


def _t5_bucket_np(dist):
    n = np.maximum(dist, 0)
    max_exact = NUM_BUCKETS // 2
    nf = np.maximum(n, 1).astype(np.float32)
    large = max_exact + (np.log(nf / np.float32(max_exact)) / np.float32(math.log(MAX_DISTANCE / max_exact))
                         * np.float32(NUM_BUCKETS - max_exact)).astype(np.int32)
    large = np.minimum(large, NUM_BUCKETS - 1)
    return np.where(n < max_exact, n, large).astype(np.int32)


def _nsa_tables(rel_bias, seq):
    TQ, TK, G = NSA_TQ, NSA_TK, NSA_GROUP
    rb = rel_bias.astype(F32) * LOG2E
    lanes = lambda a: jnp.transpose(a.reshape(a.shape[0], a.shape[1], NSA_KV_HEADS, G),
                                    (2, 0, 3, 1)).reshape(NSA_KV_HEADS, a.shape[0], G * a.shape[1])
    j = np.arange(TK)[:, None]
    i = np.arange(TQ)[None, :]
    tiles = jnp.stack([lanes(rb[_t5_bucket_np(d * TK + i - j)]) for d in range(2)], axis=1)
    far = lanes(rb[np.full((1, TQ), NUM_BUCKETS - 1, np.int32)])
    r = np.arange(_band_zero(seq) + N_CMP_PAD)[:, None]
    band = lanes(rb[_t5_bucket_np(i - (D_CMP * (r - _band_zero(seq)) + L_CMP - 1))])
    n_cmp = (seq - L_CMP) // D_CMP + 1
    n_sel = seq // L_SLC
    cmp_start = np.arange(n_cmp) * D_CMP
    sel_start = np.arange(n_sel) * L_SLC
    overlap = ((cmp_start[:, None] < sel_start[None, :] + L_SLC)
               & (cmp_start[:, None] + L_CMP > sel_start[None, :])).astype(np.float32)
    ovt = np.zeros((n_sel, N_CMP_PAD), np.float32)
    ovt[:, :n_cmp] = overlap.T
    return tiles, far, band, jnp.asarray(ovt)


def _permute_w_in(w_in):
    L, D, _ = w_in.shape
    o = np.cumsum([0, 3 * GDN_DIM, GDN_DIM, GDN_HEADS, GDN_HEADS, NSA_DIM, 6 * NSA_KV_HEADS * HEAD_DIM,
                   3 * NSA_HEADS, CONV_DIM, CONV_DIM, CONV_DIM])
    seg = lambda a, b: w_in[:, :, o[a]:o[b]]
    zeros = lambda n: jnp.zeros((L, D, n), w_in.dtype)
    n_small = 2 * GDN_HEADS + 3 * NSA_HEADS
    parts = [seg(0, 2),
             seg(4, 6),
             seg(2, 4), seg(6, 7), zeros(LANES - n_small),
             zeros(LANES),
             seg(7, 10)]
    w = jnp.concatenate(parts, axis=-1)
    assert w.shape[-1] == PROJ_PAD
    return w.astype(BF16)


def kernel(x, rel_bias, norm_mix, w_in, gdn_conv, gdn_a_log, gdn_dt_bias, gdn_norm, nsa_q_norm, nsa_k_norm,
           cmp_pos, cmp_w1, cmp_w2, sconv_w, w_out, norm_ffn, w_gate, w_up, w_down):
    B, S, D = x.shape
    depth = w_in.shape[0]
    T = B * S
    x2 = x.reshape(T, D)

    w_in_p = _permute_w_in(w_in)
    w_out_b = w_out.astype(BF16)
    w_gate_b = w_gate.astype(BF16)
    w_up_b = w_up.astype(BF16)
    w_down_b = w_down.astype(BF16)
    cmp_w1_b = cmp_w1.astype(BF16)
    cmp_w2_b = cmp_w2.astype(BF16)
    conv_t = jnp.transpose(gdn_conv, (0, 2, 1))
    sconv_t = jnp.transpose(sconv_w, (0, 2, 1))
    pad_a = lambda v: jnp.pad(v, ((0, 0), (LANE_A, LANES - LANE_A - GDN_HEADS)))[:, None, :]
    alog_vec = pad_a(gdn_a_log)
    dt_vec = pad_a(gdn_dt_bias)
    bias_tiles, far_vec, band, ovt = _nsa_tables(rel_bias, S)

    for l in range(depth):
        proj = _inproj(x2, norm_mix[l][None, :], w_in_p[l])
        y_gdn = _gdn(proj, conv_t[l], alog_vec[l], dt_vec[l], gdn_norm[l][None, :], B, S)
        kc, vc = _compress(proj, cmp_pos[l], cmp_w1_b[l], cmp_w2_b[l], nsa_k_norm[l, 0][None, :], B, S)
        y_nsa = _nsa(proj, kc, vc, nsa_q_norm[l][None, :], nsa_k_norm[l, 1:3], bias_tiles, far_vec, band,
                     ovt, B, S)
        x2 = _outproj(x2, y_gdn, y_nsa, proj, sconv_t[l], w_out_b[l], S)
        x2 = _ffn(x2, norm_ffn[l][None, :], w_gate_b[l], w_up_b[l], w_down_b[l])
    return x2.reshape(B, S, D)
```

```python
import functools
import math

import numpy as np
import jax
import jax.numpy as jnp
from jax import lax
from jax.experimental import pallas as pl
from jax.experimental.pallas import tpu as pltpu

HEAD_DIM = 128
GDN_HEADS = 6
NSA_HEADS = 6
NSA_KV_HEADS = 2
NSA_GROUP = NSA_HEADS // NSA_KV_HEADS
CONV_DIM = 512
GDN_DIM = GDN_HEADS * HEAD_DIM
NSA_DIM = NSA_HEADS * HEAD_DIM
GDN_CONV_W = 4
GDN_CHUNK = 64
L_CMP = 32
D_CMP = 16
L_SLC = 64
TOP_N = 8
WINDOW = 512
NUM_BUCKETS = 32
MAX_DISTANCE = 128
SHORT_CONV_W = 3
RMS_EPS = 1e-6
FORCED_SCORE = 1e4
NEG_BIG = -1e30
LOG2E = math.log2(math.e)

LANES = 128
SUBLANES = 8
BF16_ROWS = 16
VMEM_LIMIT_BYTES = 56 * 1024 * 1024

BLK_QKV = 0
BLK_Z = 18
BLK_NQ = 24
BLK_NKV = 30
BLK_SMALL = 42
BLK_CU = 44
BLK_CB = 48
BLK_CC = 52
N_BLK = 56
PROJ_PAD = N_BLK * LANES
LANE_BETA = 0
LANE_A = GDN_HEADS
LANE_GATE = 2 * GDN_HEADS

GDN_GROUP_ROWS = 256
GDN_HEADS_PER_PROGRAM = 6
N_CMP_PAD = 128
NSA_TQ = 256
NSA_TK = 256

F32 = jnp.float32
BF16 = jnp.bfloat16


def _band_zero(seq):
    return (seq // NSA_TQ - 1) * (NSA_TQ // D_CMP)


def _dot(a, b):
    return jnp.dot(a, b, preferred_element_type=F32)


def _dot_nt(a, b):
    return lax.dot_general(a, b, (((1,), (1,)), ((), ())), preferred_element_type=F32)


def _dot_tn(a, b):
    return lax.dot_general(a, b, (((0,), (0,)), ((), ())), preferred_element_type=F32)


def _silu(x):
    hx = 0.5 * x
    return hx + hx * jnp.tanh(hx)


def _softplus(x):
    return jnp.maximum(x, 0.0) + jnp.log1p(jnp.exp(-jnp.abs(x)))


def _lane_pick(x, lane):
    idx = lax.broadcasted_iota(jnp.int32, x.shape, 1)
    return jnp.sum(jnp.where(idx == lane, x, 0.0), axis=-1, keepdims=True)


def _round_robin(gens):
    results = [None] * len(gens)
    live = list(range(len(gens)))
    while live:
        for i in list(live):
            try:
                next(gens[i])
            except StopIteration as stop:
                results[i] = stop.value
                live.remove(i)
    return tuple(results)


def _shift_rows(cur, prev, s):
    rolled = pltpu.roll(cur, s, 0)
    rows = lax.broadcasted_iota(jnp.int32, cur.shape, 0)
    out = rolled
    for r in range(s):
        out = jnp.where(rows == r, prev[SUBLANES - s + r:SUBLANES - s + r + 1, :], out)
    return out


def _inproj_kernel(x_ref, g_ref, w_ref, o_ref, small_ref, h_ref, *, small_tile, small_off):
    j = pl.program_id(1)

    @pl.when(j == 0)
    def _():
        x = x_ref[...]
        ms = jnp.mean(x * x, axis=-1, keepdims=True)
        h_ref[...] = (x * lax.rsqrt(ms + RMS_EPS) * g_ref[...]).astype(BF16)

    acc = _dot(h_ref[...], w_ref[...])
    o_ref[...] = acc.astype(o_ref.dtype)

    @pl.when(j == small_tile)
    def _():
        small_ref[...] = acc[:, small_off:small_off + LANES]


def _inproj(x2, g, w, layer, tm=1024, tn=512):
    T, D = x2.shape
    N = w.shape[2]
    small_col = BLK_SMALL * LANES
    return pl.pallas_call(
        functools.partial(_inproj_kernel, small_tile=small_col // tn, small_off=small_col % tn),
        grid=(T // tm, N // tn),
        in_specs=[
            pl.BlockSpec((tm, D), lambda i, j: (i, 0)),
            pl.BlockSpec((1, D), lambda i, j: (0, 0)),
            pl.BlockSpec((None, D, tn), lambda i, j: (layer, 0, j)),
        ],
        out_specs=[pl.BlockSpec((tm, tn), lambda i, j: (i, j)),
                   pl.BlockSpec((tm, LANES), lambda i, j: (i, 0))],
        out_shape=[jax.ShapeDtypeStruct((T, N), BF16), jax.ShapeDtypeStruct((T, LANES), F32)],
        scratch_shapes=[pltpu.VMEM((tm, D), BF16)],
        compiler_params=pltpu.CompilerParams(
            dimension_semantics=("parallel", "arbitrary"), vmem_limit_bytes=VMEM_LIMIT_BYTES),
        name="inproj",
    )(x2, g, w)


def _gdn_kernel(q_ref, k_ref, v_ref, z_ref, small_ref, cq_ref, ck_ref, cv_ref, alog_ref, dt_ref,
                ng_ref, o_ref, conv_buf, *, seq):
    R = GDN_GROUP_ROWS
    C = GDN_CHUNK
    n_groups = seq // R
    rows = lax.broadcasted_iota(jnp.int32, (R, R), 0)
    cols = lax.broadcasted_iota(jnp.int32, (R, R), 1)
    same = (rows & -C) == (cols & -C)
    tril = same & (rows >= cols)
    strict = same & (rows > cols)
    row_in_chunk = lax.broadcasted_iota(jnp.int32, (R, LANES), 0) & (C - 1)
    neg_a = -jnp.exp(alog_ref[...])

    def conv_silu(ref, w_ref, buf, ls, r0, first):
        cur = ref[pl.ds(r0, R), ls].astype(F32)
        p0 = pl.multiple_of(jnp.maximum(r0 - BF16_ROWS, 0), BF16_ROWS)
        prev = ref[pl.ds(p0, BF16_ROWS), ls].astype(F32)[BF16_ROWS - SUBLANES:, :]
        buf[0:SUBLANES, :] = jnp.where(first, 0.0, prev)
        buf[SUBLANES:SUBLANES + R, :] = cur
        w = w_ref[:, ls]
        y = cur * w[3:4, :]
        for s in range(1, GDN_CONV_W):
            y = y + buf[SUBLANES - s:SUBLANES - s + R, :] * w[3 - s:4 - s, :]
        return _silu(y)

    def group(gi, states):
        return _round_robin([head_group(gi, states[hh], hh) for hh in range(GDN_HEADS_PER_PROGRAM)])

    def head_group(gi, state, hh):
        h = pl.program_id(1) * GDN_HEADS_PER_PROGRAM + hh
        ls = slice(hh * HEAD_DIM, (hh + 1) * HEAD_DIM)
        r0 = pl.multiple_of(gi * R, R)
        first = gi == 0
        q = conv_silu(q_ref, cq_ref, conv_buf.at[3 * hh], ls, r0, first)
        k = conv_silu(k_ref, ck_ref, conv_buf.at[3 * hh + 1], ls, r0, first)
        v = conv_silu(v_ref, cv_ref, conv_buf.at[3 * hh + 2], ls, r0, first)
        q = q * lax.rsqrt(jnp.sum(q * q, axis=-1, keepdims=True) + RMS_EPS) * (HEAD_DIM ** -0.5)
        k = k * lax.rsqrt(jnp.sum(k * k, axis=-1, keepdims=True) + RMS_EPS)
        small = small_ref[pl.ds(r0, R), :]
        beta = _lane_pick(jax.nn.sigmoid(small), LANE_BETA + h)
        g = _lane_pick(neg_a * _softplus(small + dt_ref[...]), LANE_A + h)
        gc = jnp.broadcast_to(g, (R, LANES))
        s = 1
        while s < C:
            gc = gc + jnp.where(row_in_chunk >= s, pltpu.roll(gc, s, 0), 0.0)
            s *= 2
        gc_col = jnp.concatenate([gc] * (R // LANES), axis=1)
        gc_row = gc_col.T
        decay = jnp.where(tril, jnp.exp(jnp.where(tril, gc_col - gc_row, 0.0)), 0.0)
        kb = k * beta
        vb = v * beta
        k16 = k.astype(BF16)
        yield
        lmat = jnp.where(strict, _dot_nt(kb.astype(BF16), k16) * decay, 0.0)
        attn = jnp.where(tril, _dot_nt(q.astype(BF16), k16) * decay, 0.0)
        yield
        y = -lmat
        p = lmat
        n = 1
        while 2 * n < C:
            p16 = p.astype(BF16)
            p = _dot(p16, p16)
            yield
            n *= 2
            y = y + p + _dot(y.astype(BF16), p.astype(BF16))
            yield
        eg = jnp.exp(gc)
        rhs = jnp.concatenate([vb, kb * eg], axis=1)
        sol = rhs + _dot(y.astype(BF16), rhs.astype(BF16))
        yield
        u = sol[:, :HEAD_DIM]
        w = sol[:, HEAD_DIM:]
        qg = (q * eg).astype(BF16)
        outs = []
        for c in range(R // C):
            sl = slice(c * C, (c + 1) * C)
            g_last = gc[c * C + C - 1:c * C + C, :]
            kdec = k[sl] * jnp.exp(g_last - gc[sl])
            s16 = state.astype(BF16)
            v_new = u[sl] - _dot(w[sl].astype(BF16), s16)
            yield
            vn16 = v_new.astype(BF16)
            o_c = _dot(qg[sl], s16) + _dot(attn[sl, sl].astype(BF16), vn16)
            state = state * jnp.exp(g_last[:, 0:1]) + _dot_tn(kdec.astype(BF16), vn16)
            yield
            outs.append(o_c)
        o = jnp.concatenate(outs, axis=0)
        o = o * lax.rsqrt(jnp.mean(o * o, axis=-1, keepdims=True) + RMS_EPS) * ng_ref[...]
        o_ref[pl.ds(r0, R), ls] = (o * _silu(z_ref[pl.ds(r0, R), ls].astype(F32))).astype(o_ref.dtype)
        return state

    lax.fori_loop(0, n_groups, group,
                  tuple(jnp.zeros((HEAD_DIM, HEAD_DIM), F32) for _ in range(GDN_HEADS_PER_PROGRAM)))


def _gdn(proj, small, conv_t, alog_vec, dt_vec, norm_g, layer, batch, seq):
    T = proj.shape[0]
    P = GDN_HEADS_PER_PROGRAM
    W = P * LANES
    col = lambda blk: (lambda b, h: (b, blk // P + h))
    wcol = lambda blk: (lambda b, h: (layer, 0, blk // P + h))
    const = lambda b, h: (0, 0)
    return pl.pallas_call(
        functools.partial(_gdn_kernel, seq=seq),
        grid=(batch, GDN_HEADS // P),
        in_specs=[
            pl.BlockSpec((seq, W), col(BLK_QKV)),
            pl.BlockSpec((seq, W), col(BLK_QKV + GDN_HEADS)),
            pl.BlockSpec((seq, W), col(BLK_QKV + 2 * GDN_HEADS)),
            pl.BlockSpec((seq, W), col(BLK_Z)),
            pl.BlockSpec((seq, LANES), lambda b, h: (b, 0)),
            pl.BlockSpec((None, GDN_CONV_W, W), wcol(0)),
            pl.BlockSpec((None, GDN_CONV_W, W), wcol(GDN_HEADS)),
            pl.BlockSpec((None, GDN_CONV_W, W), wcol(2 * GDN_HEADS)),
            pl.BlockSpec((1, LANES), const),
            pl.BlockSpec((1, LANES), const),
            pl.BlockSpec((1, LANES), const),
        ],
        out_specs=pl.BlockSpec((seq, W), lambda b, h: (b, h)),
        out_shape=jax.ShapeDtypeStruct((T, GDN_DIM), BF16),
        scratch_shapes=[pltpu.VMEM((3 * P, SUBLANES + GDN_GROUP_ROWS, HEAD_DIM), F32)],
        compiler_params=pltpu.CompilerParams(
            dimension_semantics=("parallel", "parallel"), vmem_limit_bytes=VMEM_LIMIT_BYTES),
        name="gdn",
    )(proj, proj, proj, proj, small, conv_t, conv_t, conv_t, alog_vec, dt_vec, norm_g)


def _cmp_kernel(kt_ref, vt_ref, pos_ref, w1_ref, w2_ref, kn_ref, kc_ref, vc_ref, tok_f32, *, seq):
    n_seg = seq // D_CMP
    out_refs = (kc_ref, vc_ref)
    for which, tok_ref in enumerate((kt_ref, vt_ref)):
        tok_f32[which] = tok_ref[...].astype(F32)
        hi = jnp.zeros((n_seg, HEAD_DIM), F32)
        lo = jnp.zeros((n_seg, HEAD_DIM), F32)
        for p in range(D_CMP):
            seg = tok_f32[which, pl.ds(p, n_seg, stride=D_CMP), :]
            a = (seg + pos_ref[which, p:p + 1, :]).astype(BF16)
            hi = hi + _dot(a, w1_ref[which, p * HEAD_DIM:(p + 1) * HEAD_DIM, :])
            b = (seg + pos_ref[which, D_CMP + p:D_CMP + p + 1, :]).astype(BF16)
            lo = lo + _dot(b, w1_ref[which, (D_CMP + p) * HEAD_DIM:(D_CMP + p + 1) * HEAD_DIM, :])
        hid = hi + pltpu.roll(lo, n_seg - 1, 0)
        out = _dot(_silu(hid).astype(BF16), w2_ref[which])
        if which == 0:
            out = out * lax.rsqrt(jnp.mean(out * out, axis=-1, keepdims=True) + RMS_EPS) * kn_ref[...]
        out_refs[which][...] = out.astype(BF16)


def _compress(proj, cmp_pos, w1, w2, kn0, layer, batch, seq):
    n_seg = seq // D_CMP
    shp = jax.ShapeDtypeStruct((batch, NSA_KV_HEADS, n_seg, HEAD_DIM), BF16)
    const3 = lambda b, h: (layer, 0, 0, 0)
    return pl.pallas_call(
        functools.partial(_cmp_kernel, seq=seq),
        grid=(batch, NSA_KV_HEADS),
        in_specs=[
            pl.BlockSpec((seq, LANES), lambda b, h: (b, BLK_NKV + h)),
            pl.BlockSpec((seq, LANES), lambda b, h: (b, BLK_NKV + NSA_KV_HEADS + h)),
            pl.BlockSpec((None, 2, L_CMP, HEAD_DIM), const3),
            pl.BlockSpec((None, 2, L_CMP * HEAD_DIM, HEAD_DIM), const3),
            pl.BlockSpec((None, 2, HEAD_DIM, HEAD_DIM), const3),
            pl.BlockSpec((1, HEAD_DIM), lambda b, h: (0, 0)),
        ],
        out_specs=[pl.BlockSpec((None, None, n_seg, HEAD_DIM), lambda b, h: (b, h, 0, 0))] * 2,
        out_shape=[shp, shp],
        scratch_shapes=[pltpu.VMEM((2, seq, HEAD_DIM), F32)],
        compiler_params=pltpu.CompilerParams(
            dimension_semantics=("parallel", "parallel"), vmem_limit_bytes=VMEM_LIMIT_BYTES),
        name="nsa_compress",
    )(proj, proj, cmp_pos, w1, w2, kn0)


def _nsa_kernel(q_ref, kc_ref, vc_ref, ks_ref, vs_ref, kw_ref, vw_ref, small_ref, qn_ref, kn_ref,
                bt_ref, cv_ref, band_ref, ovt_ref, o_ref,
                ksn_ref, vst_ref, kwn_ref, vwt_ref, am_ref, st_ref, vct_ref, *, seq):
    hkv = pl.program_id(1)
    qt = pl.program_id(2)
    TQ, TK, G = NSA_TQ, NSA_TK, NSA_GROUP
    LQ = G * TQ
    n_kt = seq // TK
    blocks_per_tile = TK // L_SLC

    @pl.when(qt == 0)
    def _():
        for kt in range(n_kt):
            rs = slice(kt * TK, (kt + 1) * TK)
            for src, dst, gi in ((ks_ref, ksn_ref, 0), (kw_ref, kwn_ref, 1)):
                x = src[rs, :].astype(F32)
                xn = x * lax.rsqrt(jnp.mean(x * x, axis=-1, keepdims=True) + RMS_EPS) * kn_ref[gi:gi + 1, :]
                dst[rs, :] = xn.astype(BF16)
            vst_ref[kt] = vs_ref[rs, :].astype(F32).T.astype(BF16)
            vwt_ref[kt] = vw_ref[rs, :].astype(F32).T.astype(BF16)
        vct_ref[...] = vc_ref[...].astype(F32).T.astype(BF16)

    scale = (HEAD_DIM ** -0.5) * LOG2E
    qs = []
    for g in range(G):
        x = q_ref[:, g * HEAD_DIM:(g + 1) * HEAD_DIM].astype(F32)
        xn = x * lax.rsqrt(jnp.mean(x * x, axis=-1, keepdims=True) + RMS_EPS) * qn_ref[...]
        qs.append((xn * scale).astype(BF16))
    q = jnp.concatenate(qs, axis=0)

    krow = lax.broadcasted_iota(jnp.int32, (TK, LQ), 0)
    qlane = lax.broadcasted_iota(jnp.int32, (TK, LQ), 1) & (TQ - 1)
    far_bias = cv_ref[...]

    def scores(k_ref, kt):
        return _dot_nt(k_ref[pl.ds(pl.multiple_of(kt * TK, TK), TK), :], q)

    def sel_rows(kt, extra):
        parts = []
        for b in range(blocks_per_tile):
            r = am_ref[pl.ds(kt * blocks_per_tile + b, 1), :] + extra
            parts.append(jnp.broadcast_to(r, (L_SLC, LQ)))
        return jnp.concatenate(parts, axis=0)

    def tile(carry, k_ref, vt_ref, kt, finish):
        s = scores(k_ref, kt)
        yield
        s = finish(s)
        m, l, acc = carry
        m_new = jnp.maximum(m, jnp.max(s, axis=0, keepdims=True))
        alpha = jnp.exp2(m - m_new)
        p = jnp.exp2(s - m_new)
        l = alpha * l + jnp.sum(p, axis=0, keepdims=True)
        pv = _dot(vt_ref[kt], p.astype(BF16))
        yield
        return m_new, l, alpha * acc + pv

    def init():
        return (jnp.full((1, LQ), NEG_BIG, F32), jnp.zeros((1, LQ), F32), jnp.zeros((HEAD_DIM, LQ), F32))

    def cmp_select_slc0():
        n_cmp = (seq - L_CMP) // D_CMP + 1
        band0 = pl.multiple_of(_band_zero(seq) - qt * (TQ // D_CMP), TQ // D_CMP)
        s = _dot_nt(kc_ref[...], q) + band_ref[pl.ds(band0, N_CMP_PAD), :]
        yield
        nrow = lax.broadcasted_iota(jnp.int32, (N_CMP_PAD, LQ), 0)
        t_lane = qt * TQ + (lax.broadcasted_iota(jnp.int32, (N_CMP_PAD, LQ), 1) & (TQ - 1))
        cmp_end = jnp.where(nrow < n_cmp, nrow * D_CMP + (L_CMP - 1), seq)
        valid = t_lane >= cmp_end
        s = jnp.where(valid, s, NEG_BIG)
        m = jnp.max(s, axis=0, keepdims=True)
        p = jnp.where(valid, jnp.exp2(s - m), 0.0)
        den = jnp.sum(p, axis=0, keepdims=True)
        p = p / jnp.where(den > 0, den, 1.0)
        o_cmp = _dot(vct_ref[...], p.astype(BF16))
        p_sum = p[:, 0:TQ]
        for g in range(1, G):
            p_sum = p_sum + p[:, g * TQ:(g + 1) * TQ]
        n_sel = seq // L_SLC
        imp = jnp.dot(ovt_ref[...], p_sum, precision=lax.Precision.HIGHEST,
                      preferred_element_type=F32)
        yield
        jb = lax.broadcasted_iota(jnp.int32, (n_sel, TQ), 0)
        cur = (qt * TQ + lax.broadcasted_iota(jnp.int32, (n_sel, TQ), 1)) >> int(math.log2(L_SLC))
        val = jnp.where(jb == 0, FORCED_SCORE, jnp.where(jb >= cur - 1, FORCED_SCORE, imp))
        val = jnp.where(jb > cur, -1.0, val)
        cnt = jnp.zeros((n_sel, TQ), F32)
        for i in range(n_sel):
            r = val[i:i + 1, :]
            cnt = cnt + jnp.where(r > val, 1.0, jnp.where(r == val, jnp.where(jb > i, 1.0, 0.0), 0.0))
        add_mask = jnp.where(cnt < min(TOP_N, n_sel), 0.0, NEG_BIG)
        am_ref[...] = jnp.concatenate([add_mask] * G, axis=1)
        diag = lambda s: jnp.where(krow <= qlane, s + bt_ref[0] + sel_rows(qt, 0.0), NEG_BIG)
        carry = yield from tile(init(), ksn_ref, vst_ref, qt, diag)
        return o_cmp, carry

    def win0():
        return (yield from tile(init(), kwn_ref, vwt_ref, qt,
                                lambda s: jnp.where(krow <= qlane, s + bt_ref[0], NEG_BIG)))

    (o_cmp, c_slc), c_win = _round_robin([cmp_select_slc0(), win0()])

    def prev_tiles(carries):
        c_slc, c_win = carries
        kt = qt - 1
        return _round_robin([
            tile(c_slc, ksn_ref, vst_ref, kt, lambda s: s + bt_ref[1] + sel_rows(kt, 0.0)),
            tile(c_win, kwn_ref, vwt_ref, kt, lambda s: s + bt_ref[1])])

    def edge_tiles(carries):
        c_slc, c_win = carries
        return _round_robin([
            tile(c_slc, ksn_ref, vst_ref, 0, lambda s: s + sel_rows(0, far_bias)),
            tile(c_win, kwn_ref, vwt_ref, qt - 2, lambda s: jnp.where(qlane < krow, s + far_bias, NEG_BIG))])

    carries = lax.cond(qt >= 1, prev_tiles, lambda c: c, (c_slc, c_win))
    c_slc, c_win = lax.cond(qt >= 2, edge_tiles, lambda c: c, carries)

    def slc_far(kt, carry):
        return _round_robin([tile(carry, ksn_ref, vst_ref, kt, lambda s: s + sel_rows(kt, far_bias))])[0]

    m, l, acc = lax.fori_loop(1, jnp.maximum(qt - 1, 1), slc_far, c_slc)
    o_slc = acc / l
    m, l, acc = c_win
    o_win = acc / l

    st_ref[...] = small_ref[...].T

    def gate_row(branch):
        rows = [st_ref[pl.ds(LANE_GATE + branch * NSA_HEADS + hkv * G + g, 1), :] for g in range(G)]
        return jax.nn.sigmoid(jnp.concatenate(rows, axis=1))

    o = gate_row(0) * o_cmp + gate_row(1) * o_slc + gate_row(2) * o_win
    for g in range(G):
        o_ref[:, g * HEAD_DIM:(g + 1) * HEAD_DIM] = o[:, g * TQ:(g + 1) * TQ].T.astype(o_ref.dtype)


def _nsa(proj, small, kc, vc, qn, kn12, bias_tiles, far_vec, band, ovt, batch, seq):
    T = proj.shape[0]
    TQ, TK, G = NSA_TQ, NSA_TK, NSA_GROUP
    nqt = seq // TQ
    n_kt = seq // TK
    gw = G * HEAD_DIM
    LQ = G * TQ
    n_sel = seq // L_SLC
    kvcol = lambda part: (lambda b, h, t: (b, BLK_NKV + part * NSA_KV_HEADS + h))
    return pl.pallas_call(
        functools.partial(_nsa_kernel, seq=seq),
        grid=(batch, NSA_KV_HEADS, nqt),
        in_specs=[
            pl.BlockSpec((TQ, gw), lambda b, h, t: (b * nqt + t, BLK_NQ * LANES // gw + h)),
            pl.BlockSpec((None, None, N_CMP_PAD, HEAD_DIM), lambda b, h, t: (b, h, 0, 0)),
            pl.BlockSpec((None, None, N_CMP_PAD, HEAD_DIM), lambda b, h, t: (b, h, 0, 0)),
            pl.BlockSpec((seq, LANES), kvcol(2)),
            pl.BlockSpec((seq, LANES), kvcol(3)),
            pl.BlockSpec((seq, LANES), kvcol(4)),
            pl.BlockSpec((seq, LANES), kvcol(5)),
            pl.BlockSpec((TQ, LANES), lambda b, h, t: (b * nqt + t, 0)),
            pl.BlockSpec((1, HEAD_DIM), lambda b, h, t: (0, 0)),
            pl.BlockSpec((2, HEAD_DIM), lambda b, h, t: (0, 0)),
            pl.BlockSpec((None, 2, TK, LQ), lambda b, h, t: (h, 0, 0, 0)),
            pl.BlockSpec((None, 1, LQ), lambda b, h, t: (h, 0, 0)),
            pl.BlockSpec((None, _band_zero(seq) + N_CMP_PAD, LQ), lambda b, h, t: (h, 0, 0)),
            pl.BlockSpec((n_sel, N_CMP_PAD), lambda b, h, t: (0, 0)),
        ],
        out_specs=pl.BlockSpec((TQ, gw), lambda b, h, t: (b * nqt + t, h)),
        out_shape=jax.ShapeDtypeStruct((T, NSA_DIM), BF16),
        scratch_shapes=[
            pltpu.VMEM((seq, HEAD_DIM), BF16),
            pltpu.VMEM((n_kt, HEAD_DIM, TK), BF16),
            pltpu.VMEM((seq, HEAD_DIM), BF16),
            pltpu.VMEM((n_kt, HEAD_DIM, TK), BF16),
            pltpu.VMEM((n_sel, LQ), F32),
            pltpu.VMEM((LANES, TQ), F32),
            pltpu.VMEM((HEAD_DIM, N_CMP_PAD), BF16),
        ],
        compiler_params=pltpu.CompilerParams(
            dimension_semantics=("parallel", "parallel", "arbitrary"), vmem_limit_bytes=VMEM_LIMIT_BYTES),
        name="nsa_attention",
    )(proj, kc, vc, proj, proj, proj, proj, small, qn, kn12, bias_tiles, far_vec, band, ovt)


def _outproj_kernel(x_ref, yg_ref, yn_ref, cu_ref, cb_ref, cc_ref, cup_ref, ccp_ref, sw_ref,
                    wg_ref, wn_ref, wc_ref, o_ref, *, tiles_per_seq):
    i = pl.program_id(0)
    v = cc_ref[...].astype(F32) * cu_ref[...].astype(F32)
    vp = (ccp_ref[...].astype(F32) * cup_ref[...].astype(F32))[BF16_ROWS - SUBLANES:, :]
    vp = jnp.where(i % tiles_per_seq == 0, 0.0, vp)
    sw = sw_ref[...]
    y = v * sw[2:3, :]
    for s in range(1, SHORT_CONV_W):
        y = y + _shift_rows(v, vp, s) * sw[2 - s:3 - s, :]
    yc = (cb_ref[...].astype(F32) * y).astype(BF16)
    o_ref[...] = (x_ref[...] + _dot(yg_ref[...], wg_ref[...]) + _dot(yn_ref[...], wn_ref[...])
                  + _dot(yc, wc_ref[...]))


def _outproj(x2, y_gdn, y_nsa, proj, sconv_t, w_out, layer, seq, tm=512):
    T, D = x2.shape
    cb = CONV_DIM // LANES
    prev = lambda blk: (lambda i: (jnp.maximum(i * (tm // BF16_ROWS) - 1, 0), blk // cb))
    return pl.pallas_call(
        functools.partial(_outproj_kernel, tiles_per_seq=seq // tm),
        grid=(T // tm,),
        in_specs=[
            pl.BlockSpec((tm, D), lambda i: (i, 0)),
            pl.BlockSpec((tm, GDN_DIM), lambda i: (i, 0)),
            pl.BlockSpec((tm, NSA_DIM), lambda i: (i, 0)),
            pl.BlockSpec((tm, CONV_DIM), lambda i: (i, BLK_CU // cb)),
            pl.BlockSpec((tm, CONV_DIM), lambda i: (i, BLK_CB // cb)),
            pl.BlockSpec((tm, CONV_DIM), lambda i: (i, BLK_CC // cb)),
            pl.BlockSpec((BF16_ROWS, CONV_DIM), prev(BLK_CU)),
            pl.BlockSpec((BF16_ROWS, CONV_DIM), prev(BLK_CC)),
            pl.BlockSpec((None, SHORT_CONV_W, CONV_DIM), lambda i: (layer, 0, 0)),
            pl.BlockSpec((None, GDN_DIM, D), lambda i: (layer, 0, 0)),
            pl.BlockSpec((None, NSA_DIM, D), lambda i: (layer, GDN_DIM // NSA_DIM, 0)),
            pl.BlockSpec((None, CONV_DIM, D), lambda i: (layer, (GDN_DIM + NSA_DIM) // CONV_DIM, 0)),
        ],
        out_specs=pl.BlockSpec((tm, D), lambda i: (i, 0)),
        out_shape=jax.ShapeDtypeStruct((T, D), F32),
        compiler_params=pltpu.CompilerParams(
            dimension_semantics=("parallel",), vmem_limit_bytes=VMEM_LIMIT_BYTES),
        name="outproj",
    )(x2, y_gdn, y_nsa, proj, proj, proj, proj, proj, sconv_t, w_out, w_out, w_out)


def _ffn_kernel(x_ref, g_ref, wg_ref, wu_ref, wd_ref, o_ref, h_ref):
    @pl.when(pl.program_id(1) == 0)
    def _():
        x = x_ref[...]
        ms = jnp.mean(x * x, axis=-1, keepdims=True)
        h_ref[...] = (x * lax.rsqrt(ms + RMS_EPS) * g_ref[...]).astype(BF16)
        o_ref[...] = x

    h = h_ref[...]
    a = _dot(h, wg_ref[...])
    b = _dot(h, wu_ref[...])
    o_ref[...] += _dot((_silu(a) * b).astype(BF16), wd_ref[...])


def _ffn(x2, g, w_gate, w_up, w_down, layer, tm=1024, tf=512):
    T, D = x2.shape
    F = w_gate.shape[2]
    return pl.pallas_call(
        _ffn_kernel,
        grid=(T // tm, F // tf),
        in_specs=[
            pl.BlockSpec((tm, D), lambda i, j: (i, 0), pipeline_mode=pl.Buffered(1)),
            pl.BlockSpec((1, D), lambda i, j: (0, 0)),
            pl.BlockSpec((None, D, tf), lambda i, j: (layer, 0, j)),
            pl.BlockSpec((None, D, tf), lambda i, j: (layer, 0, j)),
            pl.BlockSpec((None, tf, D), lambda i, j: (layer, j, 0)),
        ],
        out_specs=pl.BlockSpec((tm, D), lambda i, j: (i, 0)),
        out_shape=jax.ShapeDtypeStruct((T, D), F32),
        scratch_shapes=[pltpu.VMEM((tm, D), BF16)],
        compiler_params=pltpu.CompilerParams(
            dimension_semantics=("parallel", "arbitrary"), vmem_limit_bytes=VMEM_LIMIT_BYTES),
        name="ffn",
    )(x2, g, w_gate, w_up, w_down)


def _t5_bucket_np(dist):
    n = np.maximum(dist, 0)
    max_exact = NUM_BUCKETS // 2
    nf = np.maximum(n, 1).astype(np.float32)
    large = max_exact + (np.log(nf / np.float32(max_exact)) / np.float32(math.log(MAX_DISTANCE / max_exact))
                         * np.float32(NUM_BUCKETS - max_exact)).astype(np.int32)
    large = np.minimum(large, NUM_BUCKETS - 1)
    return np.where(n < max_exact, n, large).astype(np.int32)


def _nsa_tables(rel_bias, seq):
    TQ, TK, G = NSA_TQ, NSA_TK, NSA_GROUP
    rb = (rel_bias.astype(F32) * LOG2E).reshape(NUM_BUCKETS, NSA_KV_HEADS, 1, G, 1)

    def lanes(bucket):
        bucket = np.broadcast_to(bucket, (bucket.shape[0], TQ))
        idx = jnp.asarray(bucket.astype(np.int32))[None, :, None, :]
        out = jnp.zeros((NSA_KV_HEADS, bucket.shape[0], G, TQ), F32)
        for b in np.unique(bucket):
            out = jnp.where(idx == b, rb[b], out)
        return out.reshape(NSA_KV_HEADS, bucket.shape[0], G * TQ)

    j = np.arange(TK)[:, None]
    i = np.arange(TQ)[None, :]
    tiles = jnp.stack([lanes(_t5_bucket_np(d * TK + i - j)) for d in range(2)], axis=1)
    far = lanes(np.full((1, TQ), NUM_BUCKETS - 1, np.int32))
    r = np.arange(_band_zero(seq) + N_CMP_PAD)[:, None]
    band = lanes(_t5_bucket_np(i - (D_CMP * (r - _band_zero(seq)) + L_CMP - 1)))
    n_cmp = (seq - L_CMP) // D_CMP + 1
    n_sel = seq // L_SLC
    cmp_start = np.arange(n_cmp) * D_CMP
    sel_start = np.arange(n_sel) * L_SLC
    overlap = ((cmp_start[:, None] < sel_start[None, :] + L_SLC)
               & (cmp_start[:, None] + L_CMP > sel_start[None, :])).astype(np.float32)
    ovt = np.zeros((n_sel, N_CMP_PAD), np.float32)
    ovt[:, :n_cmp] = overlap.T
    return tiles, far, band, jnp.asarray(ovt)


def _permute_w_in(w_in):
    L, D, _ = w_in.shape
    o = np.cumsum([0, 3 * GDN_DIM, GDN_DIM, GDN_HEADS, GDN_HEADS, NSA_DIM, 6 * NSA_KV_HEADS * HEAD_DIM,
                   3 * NSA_HEADS, CONV_DIM, CONV_DIM, CONV_DIM])
    seg = lambda a, b: w_in[:, :, o[a]:o[b]]
    zeros = lambda n: jnp.zeros((L, D, n), w_in.dtype)
    n_small = 2 * GDN_HEADS + 3 * NSA_HEADS
    parts = [seg(0, 2),
             seg(4, 6),
             seg(2, 4), seg(6, 7), zeros(LANES - n_small),
             zeros(LANES),
             seg(7, 10)]
    w = jnp.concatenate(parts, axis=-1)
    assert w.shape[-1] == PROJ_PAD
    return w.astype(BF16)


def kernel(x, rel_bias, norm_mix, w_in, gdn_conv, gdn_a_log, gdn_dt_bias, gdn_norm, nsa_q_norm, nsa_k_norm,
           cmp_pos, cmp_w1, cmp_w2, sconv_w, w_out, norm_ffn, w_gate, w_up, w_down):
    B, S, D = x.shape
    depth = w_in.shape[0]
    T = B * S
    x2 = x.reshape(T, D)

    w_in_p = _permute_w_in(w_in)
    w_out_b = w_out.astype(BF16)
    w_gate_b = w_gate.astype(BF16)
    w_up_b = w_up.astype(BF16)
    w_down_b = w_down.astype(BF16)
    cmp_w1_b = cmp_w1.astype(BF16)
    cmp_w2_b = cmp_w2.astype(BF16)
    conv_t = jnp.transpose(gdn_conv, (0, 2, 1))
    sconv_t = jnp.transpose(sconv_w, (0, 2, 1))
    pad_a = lambda v: jnp.pad(v, ((0, 0), (LANE_A, LANES - LANE_A - GDN_HEADS)))[:, None, :]
    alog_vec = pad_a(gdn_a_log)
    dt_vec = pad_a(gdn_dt_bias)
    bias_tiles, far_vec, band, ovt = _nsa_tables(rel_bias, S)

    for l in range(depth):
        proj, small = _inproj(x2, norm_mix[l][None, :], w_in_p, l)
        y_gdn = _gdn(proj, small, conv_t, alog_vec[l], dt_vec[l], gdn_norm[l][None, :], l, B, S)
        kc, vc = _compress(proj, cmp_pos, cmp_w1_b, cmp_w2_b, nsa_k_norm[l, 0][None, :], l, B, S)
        y_nsa = _nsa(proj, small, kc, vc, nsa_q_norm[l][None, :], nsa_k_norm[l, 1:3], bias_tiles, far_vec,
                     band, ovt, B, S)
        x2 = _outproj(x2, y_gdn, y_nsa, proj, sconv_t, w_out_b, l, S)
        x2 = _ffn(x2, norm_ffn[l][None, :], w_gate_b, w_up_b, w_down_b, l)
    return x2.reshape(B, S, D)
```

```python
import functools
import math

import numpy as np
import jax
import jax.numpy as jnp
from jax import lax
from jax.experimental import pallas as pl
from jax.experimental.pallas import tpu as pltpu

HEAD_DIM = 128
GDN_HEADS = 6
NSA_HEADS = 6
NSA_KV_HEADS = 2
NSA_GROUP = NSA_HEADS // NSA_KV_HEADS
CONV_DIM = 512
GDN_DIM = GDN_HEADS * HEAD_DIM
NSA_DIM = NSA_HEADS * HEAD_DIM
GDN_CONV_W = 4
GDN_CHUNK = 64
L_CMP = 32
D_CMP = 16
L_SLC = 64
TOP_N = 8
WINDOW = 512
NUM_BUCKETS = 32
MAX_DISTANCE = 128
SHORT_CONV_W = 3
RMS_EPS = 1e-6
FORCED_SCORE = 1e4
NEG_BIG = -1e30
LOG2E = math.log2(math.e)

LANES = 128
SUBLANES = 8
BF16_ROWS = 16
VMEM_LIMIT_BYTES = 56 * 1024 * 1024

BLK_QKV = 0
BLK_Z = 18
BLK_NQ = 24
BLK_NKV = 30
BLK_SMALL = 42
BLK_CU = 44
BLK_CB = 48
BLK_CC = 52
N_BLK = 56
PROJ_PAD = N_BLK * LANES
LANE_BETA = 0
LANE_A = GDN_HEADS
LANE_GATE = 2 * GDN_HEADS

GDN_GROUP_ROWS = 256
GDN_HEADS_PER_PROGRAM = 6
N_CMP_PAD = 128
NSA_TQ = 256
NSA_TK = 256

F32 = jnp.float32
BF16 = jnp.bfloat16


def _band_zero(seq):
    return (seq // NSA_TQ - 1) * (NSA_TQ // D_CMP)


def _dot(a, b):
    return jnp.dot(a, b, preferred_element_type=F32)


def _dot_nt(a, b):
    return lax.dot_general(a, b, (((1,), (1,)), ((), ())), preferred_element_type=F32)


def _dot_tn(a, b):
    return lax.dot_general(a, b, (((0,), (0,)), ((), ())), preferred_element_type=F32)


def _silu(x):
    hx = 0.5 * x
    return hx + hx * jnp.tanh(hx)


def _softplus(x):
    return jnp.maximum(x, 0.0) + jnp.log1p(jnp.exp(-jnp.abs(x)))


def _lane_pick(x, lane):
    idx = lax.broadcasted_iota(jnp.int32, x.shape, 1)
    return jnp.sum(jnp.where(idx == lane, x, 0.0), axis=-1, keepdims=True)


def _round_robin(gens):
    results = [None] * len(gens)
    live = list(range(len(gens)))
    while live:
        for i in list(live):
            try:
                next(gens[i])
            except StopIteration as stop:
                results[i] = stop.value
                live.remove(i)
    return tuple(results)


def _shift_rows(cur, prev, s):
    rolled = pltpu.roll(cur, s, 0)
    rows = lax.broadcasted_iota(jnp.int32, cur.shape, 0)
    out = rolled
    for r in range(s):
        out = jnp.where(rows == r, prev[SUBLANES - s + r:SUBLANES - s + r + 1, :], out)
    return out


def _inproj_kernel(x_ref, g_ref, w_ref, o_ref, small_ref, h_ref, *, small_tile, small_off):
    j = pl.program_id(1)

    @pl.when(j == 0)
    def _():
        x = x_ref[...]
        ms = jnp.mean(x * x, axis=-1, keepdims=True)
        h_ref[...] = (x * lax.rsqrt(ms + RMS_EPS) * g_ref[...]).astype(BF16)

    acc = _dot(h_ref[...], w_ref[...])
    o_ref[...] = acc.astype(o_ref.dtype)

    @pl.when(j == small_tile)
    def _():
        small_ref[...] = acc[:, small_off:small_off + LANES]


def _inproj(x2, g, w, layer, tm=1024, tn=512):
    T, D = x2.shape
    N = w.shape[2]
    small_col = BLK_SMALL * LANES
    return pl.pallas_call(
        functools.partial(_inproj_kernel, small_tile=small_col // tn, small_off=small_col % tn),
        grid=(T // tm, N // tn),
        in_specs=[
            pl.BlockSpec((tm, D), lambda i, j: (i, 0)),
            pl.BlockSpec((1, D), lambda i, j: (0, 0)),
            pl.BlockSpec((None, D, tn), lambda i, j: (layer, 0, j)),
        ],
        out_specs=[pl.BlockSpec((tm, tn), lambda i, j: (i, j)),
                   pl.BlockSpec((tm, LANES), lambda i, j: (i, 0))],
        out_shape=[jax.ShapeDtypeStruct((T, N), BF16), jax.ShapeDtypeStruct((T, LANES), F32)],
        scratch_shapes=[pltpu.VMEM((tm, D), BF16)],
        compiler_params=pltpu.CompilerParams(
            dimension_semantics=("parallel", "arbitrary"), vmem_limit_bytes=VMEM_LIMIT_BYTES),
        name="inproj",
    )(x2, g, w)


def _gdn_kernel(q_ref, k_ref, v_ref, z_ref, small_ref, cq_ref, ck_ref, cv_ref, alog_ref, dt_ref,
                ng_ref, o_ref, conv_buf, *, seq):
    R = GDN_GROUP_ROWS
    C = GDN_CHUNK
    n_groups = seq // R
    rows = lax.broadcasted_iota(jnp.int32, (R, R), 0)
    cols = lax.broadcasted_iota(jnp.int32, (R, R), 1)
    same = (rows & -C) == (cols & -C)
    tril = same & (rows >= cols)
    strict = same & (rows > cols)
    row_in_chunk = lax.broadcasted_iota(jnp.int32, (R, LANES), 0) & (C - 1)
    neg_a = -jnp.exp(alog_ref[...])

    def conv_silu(ref, w_ref, buf, ls, r0, first):
        cur = ref[pl.ds(r0, R), ls].astype(F32)
        p0 = pl.multiple_of(jnp.maximum(r0 - BF16_ROWS, 0), BF16_ROWS)
        prev = ref[pl.ds(p0, BF16_ROWS), ls].astype(F32)[BF16_ROWS - SUBLANES:, :]
        buf[0:SUBLANES, :] = jnp.where(first, 0.0, prev)
        buf[SUBLANES:SUBLANES + R, :] = cur
        w = w_ref[:, ls]
        y = cur * w[3:4, :]
        for s in range(1, GDN_CONV_W):
            y = y + buf[SUBLANES - s:SUBLANES - s + R, :] * w[3 - s:4 - s, :]
        return _silu(y)

    def group(gi, states):
        small = small_ref[pl.ds(pl.multiple_of(gi * R, R), R), :]
        beta_all = jax.nn.sigmoid(small)
        gc_all = neg_a * _softplus(small + dt_ref[...])
        s = 1
        while s < C:
            gc_all = gc_all + jnp.where(row_in_chunk >= s, pltpu.roll(gc_all, s, 0), 0.0)
            s *= 2
        return _round_robin([head_group(gi, states[hh], hh, beta_all, gc_all)
                             for hh in range(GDN_HEADS_PER_PROGRAM)])

    def head_group(gi, state, hh, beta_all, gc_all):
        h = pl.program_id(1) * GDN_HEADS_PER_PROGRAM + hh
        ls = slice(hh * HEAD_DIM, (hh + 1) * HEAD_DIM)
        r0 = pl.multiple_of(gi * R, R)
        first = gi == 0
        q = conv_silu(q_ref, cq_ref, conv_buf.at[3 * hh], ls, r0, first)
        k = conv_silu(k_ref, ck_ref, conv_buf.at[3 * hh + 1], ls, r0, first)
        v = conv_silu(v_ref, cv_ref, conv_buf.at[3 * hh + 2], ls, r0, first)
        q = q * lax.rsqrt(jnp.sum(q * q, axis=-1, keepdims=True) + RMS_EPS) * (HEAD_DIM ** -0.5)
        k = k * lax.rsqrt(jnp.sum(k * k, axis=-1, keepdims=True) + RMS_EPS)
        beta = _lane_pick(beta_all, LANE_BETA + h)
        gc = jnp.broadcast_to(_lane_pick(gc_all, LANE_A + h), (R, LANES))
        gc_col = jnp.concatenate([gc] * (R // LANES), axis=1)
        gc_row = gc_col.T
        decay = jnp.where(tril, jnp.exp(jnp.where(tril, gc_col - gc_row, 0.0)), 0.0)
        kb = k * beta
        vb = v * beta
        k16 = k.astype(BF16)
        yield
        lmat = jnp.where(strict, _dot_nt(kb.astype(BF16), k16) * decay, 0.0)
        attn = jnp.where(tril, _dot_nt(q.astype(BF16), k16) * decay, 0.0)
        yield
        y = -lmat
        p16 = lmat.astype(BF16)
        n = 1
        while 2 * n < C:
            p = _dot(p16, p16)
            yield
            n *= 2
            p16 = p.astype(BF16)
            y = y + p + _dot(y.astype(BF16), p16)
            yield
        eg = jnp.exp(gc)
        rhs = jnp.concatenate([vb, kb * eg], axis=1)
        sol = rhs + _dot(y.astype(BF16), rhs.astype(BF16))
        yield
        u = sol[:, :HEAD_DIM]
        w = sol[:, HEAD_DIM:]
        qg = (q * eg).astype(BF16)
        outs = []
        for c in range(R // C):
            sl = slice(c * C, (c + 1) * C)
            g_last = gc[c * C + C - 1:c * C + C, :]
            kdec = k[sl] * jnp.exp(g_last - gc[sl])
            s16 = state.astype(BF16)
            v_new = u[sl] - _dot(w[sl].astype(BF16), s16)
            yield
            vn16 = v_new.astype(BF16)
            o_c = _dot(qg[sl], s16) + _dot(attn[sl, sl].astype(BF16), vn16)
            state = state * jnp.exp(g_last[:, 0:1]) + _dot_tn(kdec.astype(BF16), vn16)
            yield
            outs.append(o_c)
        o = jnp.concatenate(outs, axis=0)
        o = o * lax.rsqrt(jnp.mean(o * o, axis=-1, keepdims=True) + RMS_EPS) * ng_ref[...]
        o_ref[pl.ds(r0, R), ls] = (o * _silu(z_ref[pl.ds(r0, R), ls].astype(F32))).astype(o_ref.dtype)
        return state

    lax.fori_loop(0, n_groups, group,
                  tuple(jnp.zeros((HEAD_DIM, HEAD_DIM), F32) for _ in range(GDN_HEADS_PER_PROGRAM)))


def _gdn(proj, small, conv_t, alog_vec, dt_vec, norm_g, layer, batch, seq):
    T = proj.shape[0]
    P = GDN_HEADS_PER_PROGRAM
    W = P * LANES
    col = lambda blk: (lambda b, h: (b, blk // P + h))
    wcol = lambda blk: (lambda b, h: (layer, 0, blk // P + h))
    const = lambda b, h: (0, 0)
    return pl.pallas_call(
        functools.partial(_gdn_kernel, seq=seq),
        grid=(batch, GDN_HEADS // P),
        in_specs=[
            pl.BlockSpec((seq, W), col(BLK_QKV)),
            pl.BlockSpec((seq, W), col(BLK_QKV + GDN_HEADS)),
            pl.BlockSpec((seq, W), col(BLK_QKV + 2 * GDN_HEADS)),
            pl.BlockSpec((seq, W), col(BLK_Z)),
            pl.BlockSpec((seq, LANES), lambda b, h: (b, 0)),
            pl.BlockSpec((None, GDN_CONV_W, W), wcol(0)),
            pl.BlockSpec((None, GDN_CONV_W, W), wcol(GDN_HEADS)),
            pl.BlockSpec((None, GDN_CONV_W, W), wcol(2 * GDN_HEADS)),
            pl.BlockSpec((1, LANES), const),
            pl.BlockSpec((1, LANES), const),
            pl.BlockSpec((1, LANES), const),
        ],
        out_specs=pl.BlockSpec((seq, W), lambda b, h: (b, h)),
        out_shape=jax.ShapeDtypeStruct((T, GDN_DIM), BF16),
        scratch_shapes=[pltpu.VMEM((3 * P, SUBLANES + GDN_GROUP_ROWS, HEAD_DIM), F32)],
        compiler_params=pltpu.CompilerParams(
            dimension_semantics=("parallel", "parallel"), vmem_limit_bytes=VMEM_LIMIT_BYTES),
        name="gdn",
    )(proj, proj, proj, proj, small, conv_t, conv_t, conv_t, alog_vec, dt_vec, norm_g)


def _cmp_kernel(kt_ref, vt_ref, pos_ref, w1_ref, w2_ref, kn_ref, kc_ref, vc_ref, tok_f32, *, seq):
    n_seg = seq // D_CMP
    out_refs = (kc_ref, vc_ref)
    for which, tok_ref in enumerate((kt_ref, vt_ref)):
        tok_f32[which] = tok_ref[...].astype(F32)
        hi = jnp.zeros((n_seg, HEAD_DIM), F32)
        lo = jnp.zeros((n_seg, HEAD_DIM), F32)
        for p in range(D_CMP):
            seg = tok_f32[which, pl.ds(p, n_seg, stride=D_CMP), :]
            a = (seg + pos_ref[which, p:p + 1, :]).astype(BF16)
            hi = hi + _dot(a, w1_ref[which, p * HEAD_DIM:(p + 1) * HEAD_DIM, :])
            b = (seg + pos_ref[which, D_CMP + p:D_CMP + p + 1, :]).astype(BF16)
            lo = lo + _dot(b, w1_ref[which, (D_CMP + p) * HEAD_DIM:(D_CMP + p + 1) * HEAD_DIM, :])
        hid = hi + pltpu.roll(lo, n_seg - 1, 0)
        out = _dot(_silu(hid).astype(BF16), w2_ref[which])
        if which == 0:
            out = out * lax.rsqrt(jnp.mean(out * out, axis=-1, keepdims=True) + RMS_EPS) * kn_ref[...]
        out_refs[which][...] = out.astype(BF16)


def _compress(proj, cmp_pos, w1, w2, kn0, layer, batch, seq):
    n_seg = seq // D_CMP
    shp = jax.ShapeDtypeStruct((batch, NSA_KV_HEADS, n_seg, HEAD_DIM), BF16)
    const3 = lambda b, h: (layer, 0, 0, 0)
    return pl.pallas_call(
        functools.partial(_cmp_kernel, seq=seq),
        grid=(batch, NSA_KV_HEADS),
        in_specs=[
            pl.BlockSpec((seq, LANES), lambda b, h: (b, BLK_NKV + h)),
            pl.BlockSpec((seq, LANES), lambda b, h: (b, BLK_NKV + NSA_KV_HEADS + h)),
            pl.BlockSpec((None, 2, L_CMP, HEAD_DIM), const3),
            pl.BlockSpec((None, 2, L_CMP * HEAD_DIM, HEAD_DIM), const3),
            pl.BlockSpec((None, 2, HEAD_DIM, HEAD_DIM), const3),
            pl.BlockSpec((1, HEAD_DIM), lambda b, h: (0, 0)),
        ],
        out_specs=[pl.BlockSpec((None, None, n_seg, HEAD_DIM), lambda b, h: (b, h, 0, 0))] * 2,
        out_shape=[shp, shp],
        scratch_shapes=[pltpu.VMEM((2, seq, HEAD_DIM), F32)],
        compiler_params=pltpu.CompilerParams(
            dimension_semantics=("parallel", "parallel"), vmem_limit_bytes=VMEM_LIMIT_BYTES),
        name="nsa_compress",
    )(proj, proj, cmp_pos, w1, w2, kn0)


def _nsa_kernel(q_ref, kc_ref, vc_ref, ks_ref, vs_ref, kw_ref, vw_ref, small_ref, qn_ref, kn_ref,
                bt_ref, cv_ref, band_ref, ovt_ref, o_ref,
                ksn_ref, vst_ref, kwn_ref, vwt_ref, am_ref, st_ref, vct_ref, *, seq):
    hkv = pl.program_id(1)
    qt = pl.program_id(2)
    TQ, TK, G = NSA_TQ, NSA_TK, NSA_GROUP
    LQ = G * TQ
    n_kt = seq // TK
    blocks_per_tile = TK // L_SLC

    @pl.when(qt == 0)
    def _():
        for kt in range(n_kt):
            rs = slice(kt * TK, (kt + 1) * TK)
            for src, dst, gi in ((ks_ref, ksn_ref, 0), (kw_ref, kwn_ref, 1)):
                x = src[rs, :].astype(F32)
                xn = x * lax.rsqrt(jnp.mean(x * x, axis=-1, keepdims=True) + RMS_EPS) * kn_ref[gi:gi + 1, :]
                dst[rs, :] = xn.astype(BF16)
            vst_ref[kt] = vs_ref[rs, :].astype(F32).T.astype(BF16)
            vwt_ref[kt] = vw_ref[rs, :].astype(F32).T.astype(BF16)
        vct_ref[...] = vc_ref[...].astype(F32).T.astype(BF16)

    scale = (HEAD_DIM ** -0.5) * LOG2E
    qs = []
    for g in range(G):
        x = q_ref[:, g * HEAD_DIM:(g + 1) * HEAD_DIM].astype(F32)
        xn = x * lax.rsqrt(jnp.mean(x * x, axis=-1, keepdims=True) + RMS_EPS) * qn_ref[...]
        qs.append((xn * scale).astype(BF16))
    q = jnp.concatenate(qs, axis=0)

    krow = lax.broadcasted_iota(jnp.int32, (TK, LQ), 0)
    qlane = lax.broadcasted_iota(jnp.int32, (TK, LQ), 1) & (TQ - 1)
    far_bias = cv_ref[...]

    def scores(k_ref, kt):
        return _dot_nt(k_ref[pl.ds(pl.multiple_of(kt * TK, TK), TK), :], q)

    def sel_rows(kt, extra):
        parts = []
        for b in range(blocks_per_tile):
            r = am_ref[pl.ds(kt * blocks_per_tile + b, 1), :] + extra
            parts.append(jnp.broadcast_to(r, (L_SLC, LQ)))
        return jnp.concatenate(parts, axis=0)

    def tile(carry, k_ref, vt_ref, kt, finish):
        s = scores(k_ref, kt)
        yield
        s = finish(s)
        m, l, acc = carry
        m_new = jnp.maximum(m, jnp.max(s, axis=0, keepdims=True))
        alpha = jnp.exp2(m - m_new)
        p = jnp.exp2(s - m_new)
        l = alpha * l + jnp.sum(p, axis=0, keepdims=True)
        pv = _dot(vt_ref[kt], p.astype(BF16))
        yield
        return m_new, l, alpha * acc + pv

    def init():
        return (jnp.full((1, LQ), NEG_BIG, F32), jnp.zeros((1, LQ), F32), jnp.zeros((HEAD_DIM, LQ), F32))

    def cmp_select_slc0():
        n_cmp = (seq - L_CMP) // D_CMP + 1
        band0 = pl.multiple_of(_band_zero(seq) - qt * (TQ // D_CMP), TQ // D_CMP)
        s = _dot_nt(kc_ref[...], q) + band_ref[pl.ds(band0, N_CMP_PAD), :]
        yield
        nrow = lax.broadcasted_iota(jnp.int32, (N_CMP_PAD, LQ), 0)
        t_lane = qt * TQ + (lax.broadcasted_iota(jnp.int32, (N_CMP_PAD, LQ), 1) & (TQ - 1))
        cmp_end = jnp.where(nrow < n_cmp, nrow * D_CMP + (L_CMP - 1), seq)
        valid = t_lane >= cmp_end
        s = jnp.where(valid, s, NEG_BIG)
        m = jnp.max(s, axis=0, keepdims=True)
        p = jnp.where(valid, jnp.exp2(s - m), 0.0)
        den = jnp.sum(p, axis=0, keepdims=True)
        p = p / jnp.where(den > 0, den, 1.0)
        o_cmp = _dot(vct_ref[...], p.astype(BF16))
        p_sum = p[:, 0:TQ]
        for g in range(1, G):
            p_sum = p_sum + p[:, g * TQ:(g + 1) * TQ]
        n_sel = seq // L_SLC
        imp = jnp.dot(ovt_ref[...], p_sum, precision=lax.Precision.HIGHEST,
                      preferred_element_type=F32)
        yield
        jb = lax.broadcasted_iota(jnp.int32, (n_sel, TQ), 0)
        cur = (qt * TQ + lax.broadcasted_iota(jnp.int32, (n_sel, TQ), 1)) >> int(math.log2(L_SLC))
        val = jnp.where(jb == 0, FORCED_SCORE, jnp.where(jb >= cur - 1, FORCED_SCORE, imp))
        val = jnp.where(jb > cur, -1.0, val)
        cnt = jnp.zeros((n_sel, TQ), F32)
        for i in range(n_sel):
            r = val[i:i + 1, :]
            cnt = cnt + jnp.where(r > val, 1.0, jnp.where(r == val, jnp.where(jb > i, 1.0, 0.0), 0.0))
        add_mask = jnp.where(cnt < min(TOP_N, n_sel), 0.0, NEG_BIG)
        am_ref[...] = jnp.concatenate([add_mask] * G, axis=1)
        diag = lambda s: jnp.where(krow <= qlane, s + bt_ref[0] + sel_rows(qt, 0.0), NEG_BIG)
        carry = yield from tile(init(), ksn_ref, vst_ref, qt, diag)
        return o_cmp, carry

    def win0():
        return (yield from tile(init(), kwn_ref, vwt_ref, qt,
                                lambda s: jnp.where(krow <= qlane, s + bt_ref[0], NEG_BIG)))

    (o_cmp, c_slc), c_win = _round_robin([cmp_select_slc0(), win0()])

    def prev_tiles(carries):
        c_slc, c_win = carries
        kt = qt - 1
        return _round_robin([
            tile(c_slc, ksn_ref, vst_ref, kt, lambda s: s + bt_ref[1] + sel_rows(kt, 0.0)),
            tile(c_win, kwn_ref, vwt_ref, kt, lambda s: s + bt_ref[1])])

    def edge_tiles(carries):
        c_slc, c_win = carries
        return _round_robin([
            tile(c_slc, ksn_ref, vst_ref, 0, lambda s: s + sel_rows(0, far_bias)),
            tile(c_win, kwn_ref, vwt_ref, qt - 2, lambda s: jnp.where(qlane < krow, s + far_bias, NEG_BIG))])

    carries = lax.cond(qt >= 1, prev_tiles, lambda c: c, (c_slc, c_win))
    c_slc, c_win = lax.cond(qt >= 2, edge_tiles, lambda c: c, carries)

    def slc_far(kt, carry):
        return _round_robin([tile(carry, ksn_ref, vst_ref, kt, lambda s: s + sel_rows(kt, far_bias))])[0]

    m, l, acc = lax.fori_loop(1, jnp.maximum(qt - 1, 1), slc_far, c_slc)
    o_slc = acc / l
    m, l, acc = c_win
    o_win = acc / l

    st_ref[...] = small_ref[...].T

    def gate_row(branch):
        rows = [st_ref[pl.ds(LANE_GATE + branch * NSA_HEADS + hkv * G + g, 1), :] for g in range(G)]
        return jax.nn.sigmoid(jnp.concatenate(rows, axis=1))

    o = gate_row(0) * o_cmp + gate_row(1) * o_slc + gate_row(2) * o_win
    for g in range(G):
        o_ref[:, g * HEAD_DIM:(g + 1) * HEAD_DIM] = o[:, g * TQ:(g + 1) * TQ].T.astype(o_ref.dtype)


def _nsa(proj, small, kc, vc, qn, kn12, bias_tiles, far_vec, band, ovt, batch, seq):
    T = proj.shape[0]
    TQ, TK, G = NSA_TQ, NSA_TK, NSA_GROUP
    nqt = seq // TQ
    n_kt = seq // TK
    gw = G * HEAD_DIM
    LQ = G * TQ
    n_sel = seq // L_SLC
    kvcol = lambda part: (lambda b, h, t: (b, BLK_NKV + part * NSA_KV_HEADS + h))
    return pl.pallas_call(
        functools.partial(_nsa_kernel, seq=seq),
        grid=(batch, NSA_KV_HEADS, nqt),
        in_specs=[
            pl.BlockSpec((TQ, gw), lambda b, h, t: (b * nqt + t, BLK_NQ * LANES // gw + h)),
            pl.BlockSpec((None, None, N_CMP_PAD, HEAD_DIM), lambda b, h, t: (b, h, 0, 0)),
            pl.BlockSpec((None, None, N_CMP_PAD, HEAD_DIM), lambda b, h, t: (b, h, 0, 0)),
            pl.BlockSpec((seq, LANES), kvcol(2)),
            pl.BlockSpec((seq, LANES), kvcol(3)),
            pl.BlockSpec((seq, LANES), kvcol(4)),
            pl.BlockSpec((seq, LANES), kvcol(5)),
            pl.BlockSpec((TQ, LANES), lambda b, h, t: (b * nqt + t, 0)),
            pl.BlockSpec((1, HEAD_DIM), lambda b, h, t: (0, 0)),
            pl.BlockSpec((2, HEAD_DIM), lambda b, h, t: (0, 0)),
            pl.BlockSpec((None, 2, TK, LQ), lambda b, h, t: (h, 0, 0, 0)),
            pl.BlockSpec((None, 1, LQ), lambda b, h, t: (h, 0, 0)),
            pl.BlockSpec((None, _band_zero(seq) + N_CMP_PAD, LQ), lambda b, h, t: (h, 0, 0)),
            pl.BlockSpec((n_sel, N_CMP_PAD), lambda b, h, t: (0, 0)),
        ],
        out_specs=pl.BlockSpec((TQ, gw), lambda b, h, t: (b * nqt + t, h)),
        out_shape=jax.ShapeDtypeStruct((T, NSA_DIM), BF16),
        scratch_shapes=[
            pltpu.VMEM((seq, HEAD_DIM), BF16),
            pltpu.VMEM((n_kt, HEAD_DIM, TK), BF16),
            pltpu.VMEM((seq, HEAD_DIM), BF16),
            pltpu.VMEM((n_kt, HEAD_DIM, TK), BF16),
            pltpu.VMEM((n_sel, LQ), F32),
            pltpu.VMEM((LANES, TQ), F32),
            pltpu.VMEM((HEAD_DIM, N_CMP_PAD), BF16),
        ],
        compiler_params=pltpu.CompilerParams(
            dimension_semantics=("parallel", "parallel", "arbitrary"), vmem_limit_bytes=VMEM_LIMIT_BYTES),
        name="nsa_attention",
    )(proj, kc, vc, proj, proj, proj, proj, small, qn, kn12, bias_tiles, far_vec, band, ovt)


def _outproj_kernel(x_ref, yg_ref, yn_ref, cu_ref, cb_ref, cc_ref, cup_ref, ccp_ref, sw_ref,
                    wg_ref, wn_ref, wc_ref, o_ref, *, tiles_per_seq):
    i = pl.program_id(0)
    v = cc_ref[...].astype(F32) * cu_ref[...].astype(F32)
    vp = (ccp_ref[...].astype(F32) * cup_ref[...].astype(F32))[BF16_ROWS - SUBLANES:, :]
    vp = jnp.where(i % tiles_per_seq == 0, 0.0, vp)
    sw = sw_ref[...]
    y = v * sw[2:3, :]
    for s in range(1, SHORT_CONV_W):
        y = y + _shift_rows(v, vp, s) * sw[2 - s:3 - s, :]
    yc = (cb_ref[...].astype(F32) * y).astype(BF16)
    o_ref[...] = (x_ref[...] + _dot(yg_ref[...], wg_ref[...]) + _dot(yn_ref[...], wn_ref[...])
                  + _dot(yc, wc_ref[...]))


def _outproj(x2, y_gdn, y_nsa, proj, sconv_t, w_out, layer, seq, tm=512):
    T, D = x2.shape
    cb = CONV_DIM // LANES
    prev = lambda blk: (lambda i: (jnp.maximum(i * (tm // BF16_ROWS) - 1, 0), blk // cb))
    return pl.pallas_call(
        functools.partial(_outproj_kernel, tiles_per_seq=seq // tm),
        grid=(T // tm,),
        in_specs=[
            pl.BlockSpec((tm, D), lambda i: (i, 0)),
            pl.BlockSpec((tm, GDN_DIM), lambda i: (i, 0)),
            pl.BlockSpec((tm, NSA_DIM), lambda i: (i, 0)),
            pl.BlockSpec((tm, CONV_DIM), lambda i: (i, BLK_CU // cb)),
            pl.BlockSpec((tm, CONV_DIM), lambda i: (i, BLK_CB // cb)),
            pl.BlockSpec((tm, CONV_DIM), lambda i: (i, BLK_CC // cb)),
            pl.BlockSpec((BF16_ROWS, CONV_DIM), prev(BLK_CU)),
            pl.BlockSpec((BF16_ROWS, CONV_DIM), prev(BLK_CC)),
            pl.BlockSpec((None, SHORT_CONV_W, CONV_DIM), lambda i: (layer, 0, 0)),
            pl.BlockSpec((None, GDN_DIM, D), lambda i: (layer, 0, 0)),
            pl.BlockSpec((None, NSA_DIM, D), lambda i: (layer, GDN_DIM // NSA_DIM, 0)),
            pl.BlockSpec((None, CONV_DIM, D), lambda i: (layer, (GDN_DIM + NSA_DIM) // CONV_DIM, 0)),
        ],
        out_specs=pl.BlockSpec((tm, D), lambda i: (i, 0)),
        out_shape=jax.ShapeDtypeStruct((T, D), F32),
        compiler_params=pltpu.CompilerParams(
            dimension_semantics=("parallel",), vmem_limit_bytes=VMEM_LIMIT_BYTES),
        name="outproj",
    )(x2, y_gdn, y_nsa, proj, proj, proj, proj, proj, sconv_t, w_out, w_out, w_out)


def _ffn_kernel(x_ref, g_ref, wg_ref, wu_ref, wd_ref, o_ref, h_ref):
    @pl.when(pl.program_id(1) == 0)
    def _():
        x = x_ref[...]
        ms = jnp.mean(x * x, axis=-1, keepdims=True)
        h_ref[...] = (x * lax.rsqrt(ms + RMS_EPS) * g_ref[...]).astype(BF16)
        o_ref[...] = x

    h = h_ref[...]
    a = _dot(h, wg_ref[...])
    b = _dot(h, wu_ref[...])
    o_ref[...] += _dot((_silu(a) * b).astype(BF16), wd_ref[...])


def _ffn(x2, g, w_gate, w_up, w_down, layer, tm=1024, tf=256):
    T, D = x2.shape
    F = w_gate.shape[2]
    return pl.pallas_call(
        _ffn_kernel,
        grid=(T // tm, F // tf),
        in_specs=[
            pl.BlockSpec((tm, D), lambda i, j: (i, 0)),
            pl.BlockSpec((1, D), lambda i, j: (0, 0)),
            pl.BlockSpec((None, D, tf), lambda i, j: (layer, 0, j)),
            pl.BlockSpec((None, D, tf), lambda i, j: (layer, 0, j)),
            pl.BlockSpec((None, tf, D), lambda i, j: (layer, j, 0)),
        ],
        out_specs=pl.BlockSpec((tm, D), lambda i, j: (i, 0)),
        out_shape=jax.ShapeDtypeStruct((T, D), F32),
        scratch_shapes=[pltpu.VMEM((tm, D), BF16)],
        compiler_params=pltpu.CompilerParams(
            dimension_semantics=("parallel", "arbitrary"), vmem_limit_bytes=VMEM_LIMIT_BYTES),
        name="ffn",
    )(x2, g, w_gate, w_up, w_down)


def _t5_bucket_np(dist):
    n = np.maximum(dist, 0)
    max_exact = NUM_BUCKETS // 2
    nf = np.maximum(n, 1).astype(np.float32)
    large = max_exact + (np.log(nf / np.float32(max_exact)) / np.float32(math.log(MAX_DISTANCE / max_exact))
                         * np.float32(NUM_BUCKETS - max_exact)).astype(np.int32)
    large = np.minimum(large, NUM_BUCKETS - 1)
    return np.where(n < max_exact, n, large).astype(np.int32)


def _bias_lookup_kernel(rb_ref, idx_ref, o_ref):
    idx = idx_ref[...]
    tq = idx.shape[1]
    for h in range(NSA_HEADS):
        acc = jnp.zeros(idx.shape, F32)
        for b in range(NUM_BUCKETS):
            acc = jnp.where(idx == b, rb_ref[b * NSA_HEADS + h] * LOG2E, acc)
        g = h % NSA_GROUP
        o_ref[h // NSA_GROUP, :, g * tq:(g + 1) * tq] = acc


def _bias_lookup(rel_bias, idx):
    rows, tq = idx.shape
    return pl.pallas_call(
        _bias_lookup_kernel,
        in_specs=[pl.BlockSpec(memory_space=pltpu.SMEM),
                  pl.BlockSpec((rows, tq), lambda: (0, 0))],
        out_specs=pl.BlockSpec((NSA_KV_HEADS, rows, NSA_GROUP * tq), lambda: (0, 0, 0)),
        out_shape=jax.ShapeDtypeStruct((NSA_KV_HEADS, rows, NSA_GROUP * tq), F32),
        compiler_params=pltpu.CompilerParams(vmem_limit_bytes=VMEM_LIMIT_BYTES),
        name="bias_lookup",
    )(rel_bias.astype(F32).reshape(-1), jnp.asarray(idx))


def _nsa_tables(rel_bias, seq):
    TQ, TK, G = NSA_TQ, NSA_TK, NSA_GROUP
    j = np.arange(TK)[:, None]
    i = np.arange(TQ)[None, :]
    r = np.arange(_band_zero(seq) + N_CMP_PAD)[:, None]
    buckets = [_t5_bucket_np(0 * TK + i - j), _t5_bucket_np(1 * TK + i - j),
               np.full((SUBLANES, TQ), NUM_BUCKETS - 1, np.int32),
               _t5_bucket_np(i - (D_CMP * (r - _band_zero(seq)) + L_CMP - 1))]
    table = _bias_lookup(rel_bias, np.concatenate(buckets, axis=0).astype(np.int32))
    tiles = table[:, :2 * TK].reshape(NSA_KV_HEADS, 2, TK, G * TQ)
    far = table[:, 2 * TK:2 * TK + 1]
    band = table[:, 2 * TK + SUBLANES:]
    n_cmp = (seq - L_CMP) // D_CMP + 1
    n_sel = seq // L_SLC
    cmp_start = np.arange(n_cmp) * D_CMP
    sel_start = np.arange(n_sel) * L_SLC
    overlap = ((cmp_start[:, None] < sel_start[None, :] + L_SLC)
               & (cmp_start[:, None] + L_CMP > sel_start[None, :])).astype(np.float32)
    ovt = np.zeros((n_sel, N_CMP_PAD), np.float32)
    ovt[:, :n_cmp] = overlap.T
    return tiles, far, band, jnp.asarray(ovt)


def _permute_w_in(w_in):
    L, D, _ = w_in.shape
    o = np.cumsum([0, 3 * GDN_DIM, GDN_DIM, GDN_HEADS, GDN_HEADS, NSA_DIM, 6 * NSA_KV_HEADS * HEAD_DIM,
                   3 * NSA_HEADS, CONV_DIM, CONV_DIM, CONV_DIM])
    seg = lambda a, b: w_in[:, :, o[a]:o[b]]
    zeros = lambda n: jnp.zeros((L, D, n), w_in.dtype)
    n_small = 2 * GDN_HEADS + 3 * NSA_HEADS
    parts = [seg(0, 2),
             seg(4, 6),
             seg(2, 4), seg(6, 7), zeros(LANES - n_small),
             zeros(LANES),
             seg(7, 10)]
    w = jnp.concatenate(parts, axis=-1)
    assert w.shape[-1] == PROJ_PAD
    return w.astype(BF16)


def kernel(x, rel_bias, norm_mix, w_in, gdn_conv, gdn_a_log, gdn_dt_bias, gdn_norm, nsa_q_norm, nsa_k_norm,
           cmp_pos, cmp_w1, cmp_w2, sconv_w, w_out, norm_ffn, w_gate, w_up, w_down):
    B, S, D = x.shape
    depth = w_in.shape[0]
    T = B * S
    x2 = x.reshape(T, D)

    w_in_p = _permute_w_in(w_in)
    w_out_b = w_out.astype(BF16)
    w_gate_b = w_gate.astype(BF16)
    w_up_b = w_up.astype(BF16)
    w_down_b = w_down.astype(BF16)
    cmp_w1_b = cmp_w1.astype(BF16)
    cmp_w2_b = cmp_w2.astype(BF16)
    conv_t = jnp.transpose(gdn_conv, (0, 2, 1))
    sconv_t = jnp.transpose(sconv_w, (0, 2, 1))
    pad_a = lambda v: jnp.pad(v, ((0, 0), (LANE_A, LANES - LANE_A - GDN_HEADS)))[:, None, :]
    alog_vec = pad_a(gdn_a_log)
    dt_vec = pad_a(gdn_dt_bias)
    bias_tiles, far_vec, band, ovt = _nsa_tables(rel_bias, S)

    for l in range(depth):
        proj, small = _inproj(x2, norm_mix[l][None, :], w_in_p, l)
        y_gdn = _gdn(proj, small, conv_t, alog_vec[l], dt_vec[l], gdn_norm[l][None, :], l, B, S)
        kc, vc = _compress(proj, cmp_pos, cmp_w1_b, cmp_w2_b, nsa_k_norm[l, 0][None, :], l, B, S)
        y_nsa = _nsa(proj, small, kc, vc, nsa_q_norm[l][None, :], nsa_k_norm[l, 1:3], bias_tiles, far_vec,
                     band, ovt, B, S)
        x2 = _outproj(x2, y_gdn, y_nsa, proj, sconv_t, w_out_b, l, S)
        x2 = _ffn(x2, norm_ffn[l][None, :], w_gate_b, w_up_b, w_down_b, l)
    return x2.reshape(B, S, D)
```

```python
import functools
import math

import numpy as np
import jax
import jax.numpy as jnp
from jax import lax
from jax.experimental import pallas as pl
from jax.experimental.pallas import tpu as pltpu

HEAD_DIM = 128
GDN_HEADS = 6
NSA_HEADS = 6
NSA_KV_HEADS = 2
NSA_GROUP = NSA_HEADS // NSA_KV_HEADS
CONV_DIM = 512
GDN_DIM = GDN_HEADS * HEAD_DIM
NSA_DIM = NSA_HEADS * HEAD_DIM
GDN_CONV_W = 4
GDN_CHUNK = 64
L_CMP = 32
D_CMP = 16
L_SLC = 64
TOP_N = 8
WINDOW = 512
NUM_BUCKETS = 32
MAX_DISTANCE = 128
SHORT_CONV_W = 3
RMS_EPS = 1e-6
FORCED_SCORE = 1e4
NEG_BIG = -1e30
LOG2E = math.log2(math.e)

LANES = 128
SUBLANES = 8
BF16_ROWS = 16
VMEM_LIMIT_BYTES = 56 * 1024 * 1024

BLK_QKV = 0
BLK_Z = 18
BLK_NQ = 24
BLK_NKV = 30
BLK_SMALL = 42
BLK_CU = 44
BLK_CB = 48
BLK_CC = 52
N_BLK = 56
PROJ_PAD = N_BLK * LANES
LANE_BETA = 0
LANE_A = GDN_HEADS
LANE_GATE = 2 * GDN_HEADS

GDN_GROUP_ROWS = 256
GDN_HEADS_PER_PROGRAM = 6
N_CMP_PAD = 128
NSA_TQ = 256
NSA_TK = 256

F32 = jnp.float32
BF16 = jnp.bfloat16


def _band_zero(seq):
    return (seq // NSA_TQ - 1) * (NSA_TQ // D_CMP)


def _dot(a, b):
    return jnp.dot(a, b, preferred_element_type=F32)


def _dot_nt(a, b):
    return lax.dot_general(a, b, (((1,), (1,)), ((), ())), preferred_element_type=F32)


def _dot_tn(a, b):
    return lax.dot_general(a, b, (((0,), (0,)), ((), ())), preferred_element_type=F32)


def _silu(x):
    hx = 0.5 * x
    return hx + hx * jnp.tanh(hx)


def _softplus(x):
    return jnp.maximum(x, 0.0) + jnp.log1p(jnp.exp(-jnp.abs(x)))


def _lane_pick(x, lane):
    idx = lax.broadcasted_iota(jnp.int32, x.shape, 1)
    return jnp.sum(jnp.where(idx == lane, x, 0.0), axis=-1, keepdims=True)


def _round_robin(gens):
    results = [None] * len(gens)
    live = list(range(len(gens)))
    while live:
        for i in list(live):
            try:
                next(gens[i])
            except StopIteration as stop:
                results[i] = stop.value
                live.remove(i)
    return tuple(results)


def _shift_rows(cur, prev, s):
    rolled = pltpu.roll(cur, s, 0)
    rows = lax.broadcasted_iota(jnp.int32, cur.shape, 0)
    out = rolled
    for r in range(s):
        out = jnp.where(rows == r, prev[SUBLANES - s + r:SUBLANES - s + r + 1, :], out)
    return out


def _inproj_kernel(x_ref, g_ref, w_ref, o_ref, small_ref, h_ref, *, small_tile, small_off):
    j = pl.program_id(1)

    @pl.when(j == 0)
    def _():
        x = x_ref[...]
        ms = jnp.mean(x * x, axis=-1, keepdims=True)
        h_ref[...] = (x * lax.rsqrt(ms + RMS_EPS) * g_ref[...]).astype(BF16)

    acc = _dot(h_ref[...], w_ref[...])
    o_ref[...] = acc.astype(o_ref.dtype)

    @pl.when(j == small_tile)
    def _():
        small_ref[...] = acc[:, small_off:small_off + LANES]


def _inproj(x2, g, w, layer, tm=1024, tn=1024):
    T, D = x2.shape
    N = w.shape[2]
    small_col = BLK_SMALL * LANES
    return pl.pallas_call(
        functools.partial(_inproj_kernel, small_tile=small_col // tn, small_off=small_col % tn),
        grid=(T // tm, N // tn),
        in_specs=[
            pl.BlockSpec((tm, D), lambda i, j: (i, 0)),
            pl.BlockSpec((1, D), lambda i, j: (0, 0)),
            pl.BlockSpec((None, D, tn), lambda i, j: (layer, 0, j)),
        ],
        out_specs=[pl.BlockSpec((tm, tn), lambda i, j: (i, j)),
                   pl.BlockSpec((tm, LANES), lambda i, j: (i, 0))],
        out_shape=[jax.ShapeDtypeStruct((T, N), BF16), jax.ShapeDtypeStruct((T, LANES), F32)],
        scratch_shapes=[pltpu.VMEM((tm, D), BF16)],
        compiler_params=pltpu.CompilerParams(
            dimension_semantics=("parallel", "arbitrary"), vmem_limit_bytes=VMEM_LIMIT_BYTES),
        name="inproj",
    )(x2, g, w)


def _gdn_kernel(q_ref, k_ref, v_ref, z_ref, small_ref, cq_ref, ck_ref, cv_ref, alog_ref, dt_ref,
                ng_ref, o_ref, conv_buf, *, seq):
    R = GDN_GROUP_ROWS
    C = GDN_CHUNK
    n_groups = seq // R
    rows = lax.broadcasted_iota(jnp.int32, (R, R), 0)
    cols = lax.broadcasted_iota(jnp.int32, (R, R), 1)
    same = (rows & -C) == (cols & -C)
    tril = same & (rows >= cols)
    strict = same & (rows > cols)
    row_in_chunk = lax.broadcasted_iota(jnp.int32, (R, LANES), 0) & (C - 1)
    neg_a = -jnp.exp(alog_ref[...])

    def conv_silu(ref, w_ref, buf, ls, r0, first):
        cur = ref[pl.ds(r0, R), ls].astype(F32)
        p0 = pl.multiple_of(jnp.maximum(r0 - BF16_ROWS, 0), BF16_ROWS)
        prev = ref[pl.ds(p0, BF16_ROWS), ls].astype(F32)[BF16_ROWS - SUBLANES:, :]
        buf[0:SUBLANES, :] = jnp.where(first, 0.0, prev)
        buf[SUBLANES:SUBLANES + R, :] = cur
        w = w_ref[:, ls]
        y = cur * w[3:4, :]
        for s in range(1, GDN_CONV_W):
            y = y + buf[SUBLANES - s:SUBLANES - s + R, :] * w[3 - s:4 - s, :]
        return _silu(y)

    def group(gi, states):
        small = small_ref[pl.ds(pl.multiple_of(gi * R, R), R), :]
        beta_all = jax.nn.sigmoid(small)
        gc_all = neg_a * _softplus(small + dt_ref[...])
        s = 1
        while s < C:
            gc_all = gc_all + jnp.where(row_in_chunk >= s, pltpu.roll(gc_all, s, 0), 0.0)
            s *= 2
        return _round_robin([head_group(gi, states[hh], hh, beta_all, gc_all)
                             for hh in range(GDN_HEADS_PER_PROGRAM)])

    def head_group(gi, state, hh, beta_all, gc_all):
        h = pl.program_id(1) * GDN_HEADS_PER_PROGRAM + hh
        ls = slice(hh * HEAD_DIM, (hh + 1) * HEAD_DIM)
        r0 = pl.multiple_of(gi * R, R)
        first = gi == 0
        q = conv_silu(q_ref, cq_ref, conv_buf.at[3 * hh], ls, r0, first)
        k = conv_silu(k_ref, ck_ref, conv_buf.at[3 * hh + 1], ls, r0, first)
        v = conv_silu(v_ref, cv_ref, conv_buf.at[3 * hh + 2], ls, r0, first)
        q = q * lax.rsqrt(jnp.sum(q * q, axis=-1, keepdims=True) + RMS_EPS) * (HEAD_DIM ** -0.5)
        k = k * lax.rsqrt(jnp.sum(k * k, axis=-1, keepdims=True) + RMS_EPS)
        beta = _lane_pick(beta_all, LANE_BETA + h)
        gc = jnp.broadcast_to(_lane_pick(gc_all, LANE_A + h), (R, LANES))
        gc_col = jnp.concatenate([gc] * (R // LANES), axis=1)
        gc_row = gc_col.T
        decay = jnp.where(tril, jnp.exp(jnp.where(tril, gc_col - gc_row, 0.0)), 0.0)
        kb = k * beta
        vb = v * beta
        k16 = k.astype(BF16)
        yield
        lmat = jnp.where(strict, _dot_nt(kb.astype(BF16), k16) * decay, 0.0)
        attn = jnp.where(tril, _dot_nt(q.astype(BF16), k16) * decay, 0.0)
        yield
        y = -lmat
        p16 = lmat.astype(BF16)
        n = 1
        while 2 * n < C:
            p = _dot(p16, p16)
            yield
            n *= 2
            p16 = p.astype(BF16)
            y = y + p + _dot(y.astype(BF16), p16)
            yield
        eg = jnp.exp(gc)
        rhs = jnp.concatenate([vb, kb * eg], axis=1)
        sol = rhs + _dot(y.astype(BF16), rhs.astype(BF16))
        yield
        u = sol[:, :HEAD_DIM]
        w = sol[:, HEAD_DIM:]
        qg = (q * eg).astype(BF16)
        outs = []
        for c in range(R // C):
            sl = slice(c * C, (c + 1) * C)
            g_last = gc[c * C + C - 1:c * C + C, :]
            kdec = k[sl] * jnp.exp(g_last - gc[sl])
            s16 = state.astype(BF16)
            v_new = u[sl] - _dot(w[sl].astype(BF16), s16)
            yield
            vn16 = v_new.astype(BF16)
            o_c = _dot(qg[sl], s16) + _dot(attn[sl, sl].astype(BF16), vn16)
            state = state * jnp.exp(g_last[:, 0:1]) + _dot_tn(kdec.astype(BF16), vn16)
            yield
            outs.append(o_c)
        o = jnp.concatenate(outs, axis=0)
        o = o * lax.rsqrt(jnp.mean(o * o, axis=-1, keepdims=True) + RMS_EPS) * ng_ref[...]
        o_ref[pl.ds(r0, R), ls] = (o * _silu(z_ref[pl.ds(r0, R), ls].astype(F32))).astype(o_ref.dtype)
        return state

    lax.fori_loop(0, n_groups, group,
                  tuple(jnp.zeros((HEAD_DIM, HEAD_DIM), F32) for _ in range(GDN_HEADS_PER_PROGRAM)))


def _gdn(proj, small, conv_t, alog_vec, dt_vec, norm_g, layer, batch, seq):
    T = proj.shape[0]
    P = GDN_HEADS_PER_PROGRAM
    W = P * LANES
    col = lambda blk: (lambda b, h: (b, blk // P + h))
    wcol = lambda blk: (lambda b, h: (layer, 0, blk // P + h))
    const = lambda b, h: (0, 0)
    return pl.pallas_call(
        functools.partial(_gdn_kernel, seq=seq),
        grid=(batch, GDN_HEADS // P),
        in_specs=[
            pl.BlockSpec((seq, W), col(BLK_QKV)),
            pl.BlockSpec((seq, W), col(BLK_QKV + GDN_HEADS)),
            pl.BlockSpec((seq, W), col(BLK_QKV + 2 * GDN_HEADS)),
            pl.BlockSpec((seq, W), col(BLK_Z)),
            pl.BlockSpec((seq, LANES), lambda b, h: (b, 0)),
            pl.BlockSpec((None, GDN_CONV_W, W), wcol(0)),
            pl.BlockSpec((None, GDN_CONV_W, W), wcol(GDN_HEADS)),
            pl.BlockSpec((None, GDN_CONV_W, W), wcol(2 * GDN_HEADS)),
            pl.BlockSpec((1, LANES), const),
            pl.BlockSpec((1, LANES), const),
            pl.BlockSpec((1, LANES), const),
        ],
        out_specs=pl.BlockSpec((seq, W), lambda b, h: (b, h)),
        out_shape=jax.ShapeDtypeStruct((T, GDN_DIM), BF16),
        scratch_shapes=[pltpu.VMEM((3 * P, SUBLANES + GDN_GROUP_ROWS, HEAD_DIM), F32)],
        compiler_params=pltpu.CompilerParams(
            dimension_semantics=("parallel", "parallel"), vmem_limit_bytes=VMEM_LIMIT_BYTES),
        name="gdn",
    )(proj, proj, proj, proj, small, conv_t, conv_t, conv_t, alog_vec, dt_vec, norm_g)


def _cmp_kernel(kt_ref, vt_ref, pos_ref, w1_ref, w2_ref, kn_ref, kc_ref, vc_ref, tok_f32, *, seq):
    n_seg = seq // D_CMP
    out_refs = (kc_ref, vc_ref)
    for which, tok_ref in enumerate((kt_ref, vt_ref)):
        tok_f32[which] = tok_ref[...].astype(F32)
        hi = jnp.zeros((n_seg, HEAD_DIM), F32)
        lo = jnp.zeros((n_seg, HEAD_DIM), F32)
        for p in range(D_CMP):
            seg = tok_f32[which, pl.ds(p, n_seg, stride=D_CMP), :]
            a = (seg + pos_ref[which, p:p + 1, :]).astype(BF16)
            hi = hi + _dot(a, w1_ref[which, p * HEAD_DIM:(p + 1) * HEAD_DIM, :])
            b = (seg + pos_ref[which, D_CMP + p:D_CMP + p + 1, :]).astype(BF16)
            lo = lo + _dot(b, w1_ref[which, (D_CMP + p) * HEAD_DIM:(D_CMP + p + 1) * HEAD_DIM, :])
        hid = hi + pltpu.roll(lo, n_seg - 1, 0)
        out = _dot(_silu(hid).astype(BF16), w2_ref[which])
        if which == 0:
            out = out * lax.rsqrt(jnp.mean(out * out, axis=-1, keepdims=True) + RMS_EPS) * kn_ref[...]
        out_refs[which][...] = out.astype(BF16)


def _compress(proj, cmp_pos, w1, w2, kn0, layer, batch, seq):
    n_seg = seq // D_CMP
    shp = jax.ShapeDtypeStruct((batch, NSA_KV_HEADS, n_seg, HEAD_DIM), BF16)
    const3 = lambda b, h: (layer, 0, 0, 0)
    return pl.pallas_call(
        functools.partial(_cmp_kernel, seq=seq),
        grid=(batch, NSA_KV_HEADS),
        in_specs=[
            pl.BlockSpec((seq, LANES), lambda b, h: (b, BLK_NKV + h)),
            pl.BlockSpec((seq, LANES), lambda b, h: (b, BLK_NKV + NSA_KV_HEADS + h)),
            pl.BlockSpec((None, 2, L_CMP, HEAD_DIM), const3),
            pl.BlockSpec((None, 2, L_CMP * HEAD_DIM, HEAD_DIM), const3),
            pl.BlockSpec((None, 2, HEAD_DIM, HEAD_DIM), const3),
            pl.BlockSpec((1, HEAD_DIM), lambda b, h: (0, 0)),
        ],
        out_specs=[pl.BlockSpec((None, None, n_seg, HEAD_DIM), lambda b, h: (b, h, 0, 0))] * 2,
        out_shape=[shp, shp],
        scratch_shapes=[pltpu.VMEM((2, seq, HEAD_DIM), F32)],
        compiler_params=pltpu.CompilerParams(
            dimension_semantics=("parallel", "parallel"), vmem_limit_bytes=VMEM_LIMIT_BYTES),
        name="nsa_compress",
    )(proj, proj, cmp_pos, w1, w2, kn0)


def _nsa_kernel(q_ref, kc_ref, vc_ref, ks_ref, vs_ref, kw_ref, vw_ref, small_ref, qn_ref, kn_ref,
                bt_ref, cv_ref, band_ref, ovt_ref, o_ref,
                ksn_ref, vst_ref, kwn_ref, vwt_ref, am_ref, st_ref, vct_ref, *, seq):
    hkv = pl.program_id(1)
    qt = pl.program_id(2)
    TQ, TK, G = NSA_TQ, NSA_TK, NSA_GROUP
    LQ = G * TQ
    n_kt = seq // TK
    blocks_per_tile = TK // L_SLC

    @pl.when(qt == 0)
    def _():
        for kt in range(n_kt):
            rs = slice(kt * TK, (kt + 1) * TK)
            for src, dst, gi in ((ks_ref, ksn_ref, 0), (kw_ref, kwn_ref, 1)):
                x = src[rs, :].astype(F32)
                xn = x * lax.rsqrt(jnp.mean(x * x, axis=-1, keepdims=True) + RMS_EPS) * kn_ref[gi:gi + 1, :]
                dst[rs, :] = xn.astype(BF16)
            vst_ref[kt] = vs_ref[rs, :].astype(F32).T.astype(BF16)
            vwt_ref[kt] = vw_ref[rs, :].astype(F32).T.astype(BF16)
        vct_ref[...] = vc_ref[...].astype(F32).T.astype(BF16)

    scale = (HEAD_DIM ** -0.5) * LOG2E
    qs = []
    for g in range(G):
        x = q_ref[:, g * HEAD_DIM:(g + 1) * HEAD_DIM].astype(F32)
        xn = x * lax.rsqrt(jnp.mean(x * x, axis=-1, keepdims=True) + RMS_EPS) * qn_ref[...]
        qs.append((xn * scale).astype(BF16))
    q = jnp.concatenate(qs, axis=0)

    krow = lax.broadcasted_iota(jnp.int32, (TK, LQ), 0)
    qlane = lax.broadcasted_iota(jnp.int32, (TK, LQ), 1) & (TQ - 1)
    far_bias = cv_ref[...]

    def scores(k_ref, kt):
        return _dot_nt(k_ref[pl.ds(pl.multiple_of(kt * TK, TK), TK), :], q)

    def sel_rows(kt, extra):
        parts = []
        for b in range(blocks_per_tile):
            r = am_ref[pl.ds(kt * blocks_per_tile + b, 1), :] + extra
            parts.append(jnp.broadcast_to(r, (L_SLC, LQ)))
        return jnp.concatenate(parts, axis=0)

    def softmax_pv(carry, s, vt):
        m, l, acc = carry
        m_new = jnp.maximum(m, jnp.max(s, axis=0, keepdims=True))
        alpha = jnp.exp2(m - m_new)
        p = jnp.exp2(s - m_new)
        l = alpha * l + jnp.sum(p, axis=0, keepdims=True)
        return m_new, l, alpha * acc + _dot(vt, p.astype(BF16))

    def init():
        return (jnp.full((1, LQ), NEG_BIG, F32), jnp.zeros((1, LQ), F32), jnp.zeros((HEAD_DIM, LQ), F32))

    kt1 = jnp.maximum(qt - 1, 0)
    kt2 = jnp.maximum(qt - 2, 0)
    pen1 = jnp.where(qt >= 1, 0.0, NEG_BIG)
    pen2 = jnp.where(qt >= 2, 0.0, NEG_BIG)
    causal = krow <= qlane
    n_cmp = (seq - L_CMP) // D_CMP + 1
    n_sel = seq // L_SLC
    band0 = pl.multiple_of(_band_zero(seq) - qt * (TQ // D_CMP), TQ // D_CMP)

    s_c = _dot_nt(kc_ref[...], q) + band_ref[pl.ds(band0, N_CMP_PAD), :]
    s_w0 = scores(kwn_ref, qt)

    nrow = lax.broadcasted_iota(jnp.int32, (N_CMP_PAD, LQ), 0)
    t_lane = qt * TQ + (lax.broadcasted_iota(jnp.int32, (N_CMP_PAD, LQ), 1) & (TQ - 1))
    cmp_end = jnp.where(nrow < n_cmp, nrow * D_CMP + (L_CMP - 1), seq)
    valid = t_lane >= cmp_end
    s_c = jnp.where(valid, s_c, NEG_BIG)
    m_c = jnp.max(s_c, axis=0, keepdims=True)
    p_c = jnp.where(valid, jnp.exp2(s_c - m_c), 0.0)
    den = jnp.sum(p_c, axis=0, keepdims=True)
    p_c = p_c / jnp.where(den > 0, den, 1.0)
    s_w1 = scores(kwn_ref, kt1)

    c_win = softmax_pv(init(), jnp.where(causal, s_w0 + bt_ref[0], NEG_BIG), vwt_ref[qt])
    o_cmp = _dot(vct_ref[...], p_c.astype(BF16))
    p_sum = p_c[:, 0:TQ]
    for g in range(1, G):
        p_sum = p_sum + p_c[:, g * TQ:(g + 1) * TQ]
    imp = jnp.dot(ovt_ref[...], p_sum, precision=lax.Precision.HIGHEST,
                  preferred_element_type=F32)
    s_w2 = scores(kwn_ref, kt2)

    jb = lax.broadcasted_iota(jnp.int32, (n_sel, TQ), 0)
    cur = (qt * TQ + lax.broadcasted_iota(jnp.int32, (n_sel, TQ), 1)) >> int(math.log2(L_SLC))
    val = jnp.where(jb == 0, FORCED_SCORE, jnp.where(jb >= cur - 1, FORCED_SCORE, imp))
    val = jnp.where(jb > cur, -1.0, val)
    cnt = jnp.zeros((n_sel, TQ), F32)
    for i in range(n_sel):
        r = val[i:i + 1, :]
        cnt = cnt + jnp.where(r > val, 1.0, jnp.where(r == val, jnp.where(jb > i, 1.0, 0.0), 0.0))
    add_mask = jnp.where(cnt < min(TOP_N, n_sel), 0.0, NEG_BIG)
    am_ref[...] = jnp.concatenate([add_mask] * G, axis=1)

    c_win = softmax_pv(c_win, s_w1 + bt_ref[1] + pen1, vwt_ref[kt1])
    s_s0 = scores(ksn_ref, qt)
    c_win = softmax_pv(c_win, jnp.where(qlane < krow, s_w2 + (far_bias + pen2), NEG_BIG), vwt_ref[kt2])
    s_s1 = scores(ksn_ref, kt1)
    c_slc = softmax_pv(init(), jnp.where(causal, s_s0 + bt_ref[0] + sel_rows(qt, 0.0), NEG_BIG), vst_ref[qt])
    s_f0 = scores(ksn_ref, 0)
    c_slc = softmax_pv(c_slc, s_s1 + bt_ref[1] + sel_rows(kt1, pen1), vst_ref[kt1])
    c_slc = softmax_pv(c_slc, s_f0 + sel_rows(0, far_bias + pen2), vst_ref[0])

    def slc_far(kt, carry):
        return softmax_pv(carry, scores(ksn_ref, kt) + sel_rows(kt, far_bias), vst_ref[kt])

    m, l, acc = lax.fori_loop(1, jnp.maximum(qt - 1, 1), slc_far, c_slc)
    o_slc = acc / l
    m, l, acc = c_win
    o_win = acc / l

    st_ref[...] = small_ref[...].T

    def gate_row(branch):
        rows = [st_ref[pl.ds(LANE_GATE + branch * NSA_HEADS + hkv * G + g, 1), :] for g in range(G)]
        return jax.nn.sigmoid(jnp.concatenate(rows, axis=1))

    o = gate_row(0) * o_cmp + gate_row(1) * o_slc + gate_row(2) * o_win
    for g in range(G):
        o_ref[:, g * HEAD_DIM:(g + 1) * HEAD_DIM] = o[:, g * TQ:(g + 1) * TQ].T.astype(o_ref.dtype)


def _nsa(proj, small, kc, vc, qn, kn12, bias_tiles, far_vec, band, ovt, batch, seq):
    T = proj.shape[0]
    TQ, TK, G = NSA_TQ, NSA_TK, NSA_GROUP
    nqt = seq // TQ
    n_kt = seq // TK
    gw = G * HEAD_DIM
    LQ = G * TQ
    n_sel = seq // L_SLC
    kvcol = lambda part: (lambda b, h, t: (b, BLK_NKV + part * NSA_KV_HEADS + h))
    return pl.pallas_call(
        functools.partial(_nsa_kernel, seq=seq),
        grid=(batch, NSA_KV_HEADS, nqt),
        in_specs=[
            pl.BlockSpec((TQ, gw), lambda b, h, t: (b * nqt + t, BLK_NQ * LANES // gw + h)),
            pl.BlockSpec((None, None, N_CMP_PAD, HEAD_DIM), lambda b, h, t: (b, h, 0, 0)),
            pl.BlockSpec((None, None, N_CMP_PAD, HEAD_DIM), lambda b, h, t: (b, h, 0, 0)),
            pl.BlockSpec((seq, LANES), kvcol(2)),
            pl.BlockSpec((seq, LANES), kvcol(3)),
            pl.BlockSpec((seq, LANES), kvcol(4)),
            pl.BlockSpec((seq, LANES), kvcol(5)),
            pl.BlockSpec((TQ, LANES), lambda b, h, t: (b * nqt + t, 0)),
            pl.BlockSpec((1, HEAD_DIM), lambda b, h, t: (0, 0)),
            pl.BlockSpec((2, HEAD_DIM), lambda b, h, t: (0, 0)),
            pl.BlockSpec((None, 2, TK, LQ), lambda b, h, t: (h, 0, 0, 0)),
            pl.BlockSpec((None, 1, LQ), lambda b, h, t: (h, 0, 0)),
            pl.BlockSpec((None, _band_zero(seq) + N_CMP_PAD, LQ), lambda b, h, t: (h, 0, 0)),
            pl.BlockSpec((n_sel, N_CMP_PAD), lambda b, h, t: (0, 0)),
        ],
        out_specs=pl.BlockSpec((TQ, gw), lambda b, h, t: (b * nqt + t, h)),
        out_shape=jax.ShapeDtypeStruct((T, NSA_DIM), BF16),
        scratch_shapes=[
            pltpu.VMEM((seq, HEAD_DIM), BF16),
            pltpu.VMEM((n_kt, HEAD_DIM, TK), BF16),
            pltpu.VMEM((seq, HEAD_DIM), BF16),
            pltpu.VMEM((n_kt, HEAD_DIM, TK), BF16),
            pltpu.VMEM((n_sel, LQ), F32),
            pltpu.VMEM((LANES, TQ), F32),
            pltpu.VMEM((HEAD_DIM, N_CMP_PAD), BF16),
        ],
        compiler_params=pltpu.CompilerParams(
            dimension_semantics=("parallel", "parallel", "arbitrary"), vmem_limit_bytes=VMEM_LIMIT_BYTES),
        name="nsa_attention",
    )(proj, kc, vc, proj, proj, proj, proj, small, qn, kn12, bias_tiles, far_vec, band, ovt)


def _outproj_kernel(x_ref, yg_ref, yn_ref, cu_ref, cb_ref, cc_ref, cup_ref, ccp_ref, sw_ref,
                    wg_ref, wn_ref, wc_ref, o_ref, *, tiles_per_seq):
    i = pl.program_id(0)
    v = cc_ref[...].astype(F32) * cu_ref[...].astype(F32)
    vp = (ccp_ref[...].astype(F32) * cup_ref[...].astype(F32))[BF16_ROWS - SUBLANES:, :]
    vp = jnp.where(i % tiles_per_seq == 0, 0.0, vp)
    sw = sw_ref[...]
    y = v * sw[2:3, :]
    for s in range(1, SHORT_CONV_W):
        y = y + _shift_rows(v, vp, s) * sw[2 - s:3 - s, :]
    yc = (cb_ref[...].astype(F32) * y).astype(BF16)
    o_ref[...] = (x_ref[...] + _dot(yg_ref[...], wg_ref[...]) + _dot(yn_ref[...], wn_ref[...])
                  + _dot(yc, wc_ref[...]))


def _outproj(x2, y_gdn, y_nsa, proj, sconv_t, w_out, layer, seq, tm=512):
    T, D = x2.shape
    cb = CONV_DIM // LANES
    prev = lambda blk: (lambda i: (jnp.maximum(i * (tm // BF16_ROWS) - 1, 0), blk // cb))
    return pl.pallas_call(
        functools.partial(_outproj_kernel, tiles_per_seq=seq // tm),
        grid=(T // tm,),
        in_specs=[
            pl.BlockSpec((tm, D), lambda i: (i, 0)),
            pl.BlockSpec((tm, GDN_DIM), lambda i: (i, 0)),
            pl.BlockSpec((tm, NSA_DIM), lambda i: (i, 0)),
            pl.BlockSpec((tm, CONV_DIM), lambda i: (i, BLK_CU // cb)),
            pl.BlockSpec((tm, CONV_DIM), lambda i: (i, BLK_CB // cb)),
            pl.BlockSpec((tm, CONV_DIM), lambda i: (i, BLK_CC // cb)),
            pl.BlockSpec((BF16_ROWS, CONV_DIM), prev(BLK_CU)),
            pl.BlockSpec((BF16_ROWS, CONV_DIM), prev(BLK_CC)),
            pl.BlockSpec((None, SHORT_CONV_W, CONV_DIM), lambda i: (layer, 0, 0)),
            pl.BlockSpec((None, GDN_DIM, D), lambda i: (layer, 0, 0)),
            pl.BlockSpec((None, NSA_DIM, D), lambda i: (layer, GDN_DIM // NSA_DIM, 0)),
            pl.BlockSpec((None, CONV_DIM, D), lambda i: (layer, (GDN_DIM + NSA_DIM) // CONV_DIM, 0)),
        ],
        out_specs=pl.BlockSpec((tm, D), lambda i: (i, 0)),
        out_shape=jax.ShapeDtypeStruct((T, D), F32),
        compiler_params=pltpu.CompilerParams(
            dimension_semantics=("parallel",), vmem_limit_bytes=VMEM_LIMIT_BYTES),
        name="outproj",
    )(x2, y_gdn, y_nsa, proj, proj, proj, proj, proj, sconv_t, w_out, w_out, w_out)


def _ffn_kernel(x_ref, g_ref, wg_ref, wu_ref, wd_ref, o_ref, h_ref):
    @pl.when(pl.program_id(1) == 0)
    def _():
        x = x_ref[...]
        ms = jnp.mean(x * x, axis=-1, keepdims=True)
        h_ref[...] = (x * lax.rsqrt(ms + RMS_EPS) * g_ref[...]).astype(BF16)
        o_ref[...] = x

    h = h_ref[...]
    a = _dot(h, wg_ref[...])
    b = _dot(h, wu_ref[...])
    o_ref[...] += _dot((_silu(a) * b).astype(BF16), wd_ref[...])


def _ffn(x2, g, w_gate, w_up, w_down, layer, tm=1024, tf=256):
    T, D = x2.shape
    F = w_gate.shape[2]
    return pl.pallas_call(
        _ffn_kernel,
        grid=(T // tm, F // tf),
        in_specs=[
            pl.BlockSpec((tm, D), lambda i, j: (i, 0)),
            pl.BlockSpec((1, D), lambda i, j: (0, 0)),
            pl.BlockSpec((None, D, tf), lambda i, j: (layer, 0, j)),
            pl.BlockSpec((None, D, tf), lambda i, j: (layer, 0, j)),
            pl.BlockSpec((None, tf, D), lambda i, j: (layer, j, 0)),
        ],
        out_specs=pl.BlockSpec((tm, D), lambda i, j: (i, 0)),
        out_shape=jax.ShapeDtypeStruct((T, D), F32),
        scratch_shapes=[pltpu.VMEM((tm, D), BF16)],
        compiler_params=pltpu.CompilerParams(
            dimension_semantics=("parallel", "arbitrary"), vmem_limit_bytes=VMEM_LIMIT_BYTES),
        name="ffn",
    )(x2, g, w_gate, w_up, w_down)


def _t5_bucket_np(dist):
    n = np.maximum(dist, 0)
    max_exact = NUM_BUCKETS // 2
    nf = np.maximum(n, 1).astype(np.float32)
    large = max_exact + (np.log(nf / np.float32(max_exact)) / np.float32(math.log(MAX_DISTANCE / max_exact))
                         * np.float32(NUM_BUCKETS - max_exact)).astype(np.int32)
    large = np.minimum(large, NUM_BUCKETS - 1)
    return np.where(n < max_exact, n, large).astype(np.int32)


def _bias_lookup_kernel(rb_ref, idx_ref, o_ref):
    idx = idx_ref[...]
    tq = idx.shape[1]
    for h in range(NSA_HEADS):
        acc = jnp.zeros(idx.shape, F32)
        for b in range(NUM_BUCKETS):
            acc = jnp.where(idx == b, rb_ref[b * NSA_HEADS + h] * LOG2E, acc)
        g = h % NSA_GROUP
        o_ref[h // NSA_GROUP, :, g * tq:(g + 1) * tq] = acc


def _bias_lookup(rel_bias, idx):
    rows, tq = idx.shape
    return pl.pallas_call(
        _bias_lookup_kernel,
        in_specs=[pl.BlockSpec(memory_space=pltpu.SMEM),
                  pl.BlockSpec((rows, tq), lambda: (0, 0))],
        out_specs=pl.BlockSpec((NSA_KV_HEADS, rows, NSA_GROUP * tq), lambda: (0, 0, 0)),
        out_shape=jax.ShapeDtypeStruct((NSA_KV_HEADS, rows, NSA_GROUP * tq), F32),
        compiler_params=pltpu.CompilerParams(vmem_limit_bytes=VMEM_LIMIT_BYTES),
        name="bias_lookup",
    )(rel_bias.astype(F32).reshape(-1), jnp.asarray(idx))


def _nsa_tables(rel_bias, seq):
    TQ, TK, G = NSA_TQ, NSA_TK, NSA_GROUP
    j = np.arange(TK)[:, None]
    i = np.arange(TQ)[None, :]
    r = np.arange(_band_zero(seq) + N_CMP_PAD)[:, None]
    buckets = [_t5_bucket_np(0 * TK + i - j), _t5_bucket_np(1 * TK + i - j),
               np.full((SUBLANES, TQ), NUM_BUCKETS - 1, np.int32),
               _t5_bucket_np(i - (D_CMP * (r - _band_zero(seq)) + L_CMP - 1))]
    table = _bias_lookup(rel_bias, np.concatenate(buckets, axis=0).astype(np.int32))
    tiles = table[:, :2 * TK].reshape(NSA_KV_HEADS, 2, TK, G * TQ)
    far = table[:, 2 * TK:2 * TK + 1]
    band = table[:, 2 * TK + SUBLANES:]
    n_cmp = (seq - L_CMP) // D_CMP + 1
    n_sel = seq // L_SLC
    cmp_start = np.arange(n_cmp) * D_CMP
    sel_start = np.arange(n_sel) * L_SLC
    overlap = ((cmp_start[:, None] < sel_start[None, :] + L_SLC)
               & (cmp_start[:, None] + L_CMP > sel_start[None, :])).astype(np.float32)
    ovt = np.zeros((n_sel, N_CMP_PAD), np.float32)
    ovt[:, :n_cmp] = overlap.T
    return tiles, far, band, jnp.asarray(ovt)


def _permute_w_in(w_in):
    L, D, _ = w_in.shape
    o = np.cumsum([0, 3 * GDN_DIM, GDN_DIM, GDN_HEADS, GDN_HEADS, NSA_DIM, 6 * NSA_KV_HEADS * HEAD_DIM,
                   3 * NSA_HEADS, CONV_DIM, CONV_DIM, CONV_DIM])
    seg = lambda a, b: w_in[:, :, o[a]:o[b]]
    zeros = lambda n: jnp.zeros((L, D, n), w_in.dtype)
    n_small = 2 * GDN_HEADS + 3 * NSA_HEADS
    parts = [seg(0, 2),
             seg(4, 6),
             seg(2, 4), seg(6, 7), zeros(LANES - n_small),
             zeros(LANES),
             seg(7, 10)]
    w = jnp.concatenate(parts, axis=-1)
    assert w.shape[-1] == PROJ_PAD
    return w.astype(BF16)


def kernel(x, rel_bias, norm_mix, w_in, gdn_conv, gdn_a_log, gdn_dt_bias, gdn_norm, nsa_q_norm, nsa_k_norm,
           cmp_pos, cmp_w1, cmp_w2, sconv_w, w_out, norm_ffn, w_gate, w_up, w_down):
    B, S, D = x.shape
    depth = w_in.shape[0]
    T = B * S
    x2 = x.reshape(T, D)

    w_in_p = _permute_w_in(w_in)
    w_out_b = w_out.astype(BF16)
    w_gate_b = w_gate.astype(BF16)
    w_up_b = w_up.astype(BF16)
    w_down_b = w_down.astype(BF16)
    cmp_w1_b = cmp_w1.astype(BF16)
    cmp_w2_b = cmp_w2.astype(BF16)
    conv_t = jnp.transpose(gdn_conv, (0, 2, 1))
    sconv_t = jnp.transpose(sconv_w, (0, 2, 1))
    pad_a = lambda v: jnp.pad(v, ((0, 0), (LANE_A, LANES - LANE_A - GDN_HEADS)))[:, None, :]
    alog_vec = pad_a(gdn_a_log)
    dt_vec = pad_a(gdn_dt_bias)
    bias_tiles, far_vec, band, ovt = _nsa_tables(rel_bias, S)

    for l in range(depth):
        proj, small = _inproj(x2, norm_mix[l][None, :], w_in_p, l)
        y_gdn = _gdn(proj, small, conv_t, alog_vec[l], dt_vec[l], gdn_norm[l][None, :], l, B, S)
        kc, vc = _compress(proj, cmp_pos, cmp_w1_b, cmp_w2_b, nsa_k_norm[l, 0][None, :], l, B, S)
        y_nsa = _nsa(proj, small, kc, vc, nsa_q_norm[l][None, :], nsa_k_norm[l, 1:3], bias_tiles, far_vec,
                     band, ovt, B, S)
        x2 = _outproj(x2, y_gdn, y_nsa, proj, sconv_t, w_out_b, l, S)
        x2 = _ffn(x2, norm_ffn[l][None, :], w_gate_b, w_up_b, w_down_b, l)
    return x2.reshape(B, S, D)
```

```python
import functools
import math

import numpy as np
import jax
import jax.numpy as jnp
from jax import lax
from jax.experimental import pallas as pl
from jax.experimental.pallas import tpu as pltpu

HEAD_DIM = 128
GDN_HEADS = 6
NSA_HEADS = 6
NSA_KV_HEADS = 2
NSA_GROUP = NSA_HEADS // NSA_KV_HEADS
CONV_DIM = 512
GDN_DIM = GDN_HEADS * HEAD_DIM
NSA_DIM = NSA_HEADS * HEAD_DIM
GDN_CONV_W = 4
GDN_CHUNK = 64
L_CMP = 32
D_CMP = 16
L_SLC = 64
TOP_N = 8
WINDOW = 512
NUM_BUCKETS = 32
MAX_DISTANCE = 128
SHORT_CONV_W = 3
RMS_EPS = 1e-6
FORCED_SCORE = 1e4
NEG_BIG = -1e30
LOG2E = math.log2(math.e)

LANES = 128
SUBLANES = 8
BF16_ROWS = 16
VMEM_LIMIT_BYTES = 56 * 1024 * 1024

BLK_QKV = 0
BLK_Z = 18
BLK_NQ = 24
BLK_NKV = 30
BLK_SMALL = 42
BLK_CU = 44
BLK_CB = 48
BLK_CC = 52
N_BLK = 56
PROJ_PAD = N_BLK * LANES
LANE_BETA = 0
LANE_A = GDN_HEADS
LANE_GATE = 2 * GDN_HEADS

GDN_GROUP_ROWS = 256
GDN_HEADS_PER_PROGRAM = 6
N_CMP_PAD = 128
NSA_TQ = 256
NSA_TK = 256

F32 = jnp.float32
BF16 = jnp.bfloat16


def _band_zero(seq):
    return (seq // NSA_TQ - 1) * (NSA_TQ // D_CMP)


def _dot(a, b):
    return jnp.dot(a, b, preferred_element_type=F32)


def _dot_nt(a, b):
    return lax.dot_general(a, b, (((1,), (1,)), ((), ())), preferred_element_type=F32)


def _dot_tn(a, b):
    return lax.dot_general(a, b, (((0,), (0,)), ((), ())), preferred_element_type=F32)


def _silu(x):
    hx = 0.5 * x
    return hx + hx * jnp.tanh(hx)


def _softplus(x):
    return jnp.maximum(x, 0.0) + jnp.log1p(jnp.exp(-jnp.abs(x)))


def _lane_pick(x, lane):
    idx = lax.broadcasted_iota(jnp.int32, x.shape, 1)
    return jnp.sum(jnp.where(idx == lane, x, 0.0), axis=-1, keepdims=True)


def _round_robin(gens):
    results = [None] * len(gens)
    live = list(range(len(gens)))
    while live:
        for i in list(live):
            try:
                next(gens[i])
            except StopIteration as stop:
                results[i] = stop.value
                live.remove(i)
    return tuple(results)


def _shift_rows(cur, prev, s):
    rolled = pltpu.roll(cur, s, 0)
    rows = lax.broadcasted_iota(jnp.int32, cur.shape, 0)
    out = rolled
    for r in range(s):
        out = jnp.where(rows == r, prev[SUBLANES - s + r:SUBLANES - s + r + 1, :], out)
    return out


def _inproj_kernel(x_ref, g_ref, w_ref, o_ref, small_ref, h_ref, *, small_tile, small_off):
    j = pl.program_id(1)

    @pl.when(j == 0)
    def _():
        x = x_ref[...]
        ms = jnp.mean(x * x, axis=-1, keepdims=True)
        h_ref[...] = (x * lax.rsqrt(ms + RMS_EPS) * g_ref[...]).astype(BF16)

    acc = _dot(h_ref[...], w_ref[...])
    o_ref[...] = acc.astype(o_ref.dtype)

    @pl.when(j == small_tile)
    def _():
        small_ref[...] = acc[:, small_off:small_off + LANES]


def _inproj(x2, g, w, layer, tm=1024, tn=1024):
    T, D = x2.shape
    N = w.shape[2]
    small_col = BLK_SMALL * LANES
    return pl.pallas_call(
        functools.partial(_inproj_kernel, small_tile=small_col // tn, small_off=small_col % tn),
        grid=(T // tm, N // tn),
        in_specs=[
            pl.BlockSpec((tm, D), lambda i, j: (i, 0)),
            pl.BlockSpec((1, D), lambda i, j: (0, 0)),
            pl.BlockSpec((None, D, tn), lambda i, j: (layer, 0, j)),
        ],
        out_specs=[pl.BlockSpec((tm, tn), lambda i, j: (i, j)),
                   pl.BlockSpec((tm, LANES), lambda i, j: (i, 0))],
        out_shape=[jax.ShapeDtypeStruct((T, N), BF16), jax.ShapeDtypeStruct((T, LANES), F32)],
        scratch_shapes=[pltpu.VMEM((tm, D), BF16)],
        compiler_params=pltpu.CompilerParams(
            dimension_semantics=("parallel", "arbitrary"), vmem_limit_bytes=VMEM_LIMIT_BYTES),
        name="inproj",
    )(x2, g, w)


def _gdn_kernel(q_ref, k_ref, v_ref, z_ref, small_ref, cq_ref, ck_ref, cv_ref, alog_ref, dt_ref,
                ng_ref, o_ref, conv_buf, *, seq):
    R = GDN_GROUP_ROWS
    C = GDN_CHUNK
    n_groups = seq // R
    rows = lax.broadcasted_iota(jnp.int32, (R, R), 0)
    cols = lax.broadcasted_iota(jnp.int32, (R, R), 1)
    same = (rows & -C) == (cols & -C)
    tril = same & (rows >= cols)
    strict = same & (rows > cols)
    row_in_chunk = lax.broadcasted_iota(jnp.int32, (R, LANES), 0) & (C - 1)
    neg_a = -jnp.exp(alog_ref[...])

    def conv_silu(ref, w_ref, buf, ls, r0, first):
        cur = ref[pl.ds(r0, R), ls].astype(F32)
        p0 = pl.multiple_of(jnp.maximum(r0 - BF16_ROWS, 0), BF16_ROWS)
        prev = ref[pl.ds(p0, BF16_ROWS), ls].astype(F32)[BF16_ROWS - SUBLANES:, :]
        buf[0:SUBLANES, :] = jnp.where(first, 0.0, prev)
        buf[SUBLANES:SUBLANES + R, :] = cur
        w = w_ref[:, ls]
        y = cur * w[3:4, :]
        for s in range(1, GDN_CONV_W):
            y = y + buf[SUBLANES - s:SUBLANES - s + R, :] * w[3 - s:4 - s, :]
        return _silu(y)

    def group(gi, states):
        small = small_ref[pl.ds(pl.multiple_of(gi * R, R), R), :]
        beta_all = jax.nn.sigmoid(small)
        gc_all = neg_a * _softplus(small + dt_ref[...])
        s = 1
        while s < C:
            gc_all = gc_all + jnp.where(row_in_chunk >= s, pltpu.roll(gc_all, s, 0), 0.0)
            s *= 2
        return _round_robin([head_group(gi, states[hh], hh, beta_all, gc_all)
                             for hh in range(GDN_HEADS_PER_PROGRAM)])

    def head_group(gi, state, hh, beta_all, gc_all):
        h = pl.program_id(1) * GDN_HEADS_PER_PROGRAM + hh
        ls = slice(hh * HEAD_DIM, (hh + 1) * HEAD_DIM)
        r0 = pl.multiple_of(gi * R, R)
        first = gi == 0
        q = conv_silu(q_ref, cq_ref, conv_buf.at[3 * hh], ls, r0, first)
        k = conv_silu(k_ref, ck_ref, conv_buf.at[3 * hh + 1], ls, r0, first)
        v = conv_silu(v_ref, cv_ref, conv_buf.at[3 * hh + 2], ls, r0, first)
        q = q * lax.rsqrt(jnp.sum(q * q, axis=-1, keepdims=True) + RMS_EPS) * (HEAD_DIM ** -0.5)
        k = k * lax.rsqrt(jnp.sum(k * k, axis=-1, keepdims=True) + RMS_EPS)
        beta = _lane_pick(beta_all, LANE_BETA + h)
        gc = jnp.broadcast_to(_lane_pick(gc_all, LANE_A + h), (R, LANES))
        gc_col = jnp.concatenate([gc] * (R // LANES), axis=1)
        gc_row = gc_col.T
        decay = jnp.where(tril, jnp.exp(jnp.where(tril, gc_col - gc_row, 0.0)), 0.0)
        kb = k * beta
        vb = v * beta
        k16 = k.astype(BF16)
        yield
        lmat = jnp.where(strict, _dot_nt(kb.astype(BF16), k16) * decay, 0.0)
        attn = jnp.where(tril, _dot_nt(q.astype(BF16), k16) * decay, 0.0)
        yield
        y = -lmat
        p16 = lmat.astype(BF16)
        n = 1
        while 2 * n < C:
            p = _dot(p16, p16)
            yield
            n *= 2
            p16 = p.astype(BF16)
            y = y + p + _dot(y.astype(BF16), p16)
            yield
        eg = jnp.exp(gc)
        rhs = jnp.concatenate([vb, kb * eg], axis=1)
        sol = rhs + _dot(y.astype(BF16), rhs.astype(BF16))
        yield
        u = sol[:, :HEAD_DIM]
        w = sol[:, HEAD_DIM:]
        qg = (q * eg).astype(BF16)
        outs = []
        for c in range(R // C):
            sl = slice(c * C, (c + 1) * C)
            g_last = gc[c * C + C - 1:c * C + C, :]
            kdec = k[sl] * jnp.exp(g_last - gc[sl])
            s16 = state.astype(BF16)
            v_new = u[sl] - _dot(w[sl].astype(BF16), s16)
            yield
            vn16 = v_new.astype(BF16)
            o_c = _dot(qg[sl], s16) + _dot(attn[sl, sl].astype(BF16), vn16)
            state = state * jnp.exp(g_last[:, 0:1]) + _dot_tn(kdec.astype(BF16), vn16)
            yield
            outs.append(o_c)
        o = jnp.concatenate(outs, axis=0)
        o = o * lax.rsqrt(jnp.mean(o * o, axis=-1, keepdims=True) + RMS_EPS) * ng_ref[...]
        o_ref[pl.ds(r0, R), ls] = (o * _silu(z_ref[pl.ds(r0, R), ls].astype(F32))).astype(o_ref.dtype)
        return state

    lax.fori_loop(0, n_groups, group,
                  tuple(jnp.zeros((HEAD_DIM, HEAD_DIM), F32) for _ in range(GDN_HEADS_PER_PROGRAM)))


def _gdn(proj, small, conv_t, alog_vec, dt_vec, norm_g, layer, batch, seq):
    T = proj.shape[0]
    P = GDN_HEADS_PER_PROGRAM
    W = P * LANES
    col = lambda blk: (lambda b, h: (b, blk // P + h))
    wcol = lambda blk: (lambda b, h: (layer, 0, blk // P + h))
    const = lambda b, h: (0, 0)
    return pl.pallas_call(
        functools.partial(_gdn_kernel, seq=seq),
        grid=(batch, GDN_HEADS // P),
        in_specs=[
            pl.BlockSpec((seq, W), col(BLK_QKV)),
            pl.BlockSpec((seq, W), col(BLK_QKV + GDN_HEADS)),
            pl.BlockSpec((seq, W), col(BLK_QKV + 2 * GDN_HEADS)),
            pl.BlockSpec((seq, W), col(BLK_Z)),
            pl.BlockSpec((seq, LANES), lambda b, h: (b, 0)),
            pl.BlockSpec((None, GDN_CONV_W, W), wcol(0)),
            pl.BlockSpec((None, GDN_CONV_W, W), wcol(GDN_HEADS)),
            pl.BlockSpec((None, GDN_CONV_W, W), wcol(2 * GDN_HEADS)),
            pl.BlockSpec((1, LANES), const),
            pl.BlockSpec((1, LANES), const),
            pl.BlockSpec((1, LANES), const),
        ],
        out_specs=pl.BlockSpec((seq, W), lambda b, h: (b, h)),
        out_shape=jax.ShapeDtypeStruct((T, GDN_DIM), BF16),
        scratch_shapes=[pltpu.VMEM((3 * P, SUBLANES + GDN_GROUP_ROWS, HEAD_DIM), F32)],
        compiler_params=pltpu.CompilerParams(
            dimension_semantics=("parallel", "parallel"), vmem_limit_bytes=VMEM_LIMIT_BYTES),
        name="gdn",
    )(proj, proj, proj, proj, small, conv_t, conv_t, conv_t, alog_vec, dt_vec, norm_g)


def _cmp_kernel(kt_ref, vt_ref, pos_ref, w1_ref, w2_ref, kn_ref, kc_ref, vc_ref, tok_f32, *, seq):
    n_seg = seq // D_CMP
    out_refs = (kc_ref, vc_ref)
    for which, tok_ref in enumerate((kt_ref, vt_ref)):
        tok_f32[which] = tok_ref[...].astype(F32)
        hi = jnp.zeros((n_seg, HEAD_DIM), F32)
        lo = jnp.zeros((n_seg, HEAD_DIM), F32)
        for p in range(D_CMP):
            seg = tok_f32[which, pl.ds(p, n_seg, stride=D_CMP), :]
            a = (seg + pos_ref[which, p:p + 1, :]).astype(BF16)
            hi = hi + _dot(a, w1_ref[which, p * HEAD_DIM:(p + 1) * HEAD_DIM, :])
            b = (seg + pos_ref[which, D_CMP + p:D_CMP + p + 1, :]).astype(BF16)
            lo = lo + _dot(b, w1_ref[which, (D_CMP + p) * HEAD_DIM:(D_CMP + p + 1) * HEAD_DIM, :])
        hid = hi + pltpu.roll(lo, n_seg - 1, 0)
        out = _dot(_silu(hid).astype(BF16), w2_ref[which])
        if which == 0:
            out = out * lax.rsqrt(jnp.mean(out * out, axis=-1, keepdims=True) + RMS_EPS) * kn_ref[...]
        out_refs[which][...] = out.astype(BF16)


def _compress(proj, cmp_pos, w1, w2, kn0, layer, batch, seq):
    n_seg = seq // D_CMP
    shp = jax.ShapeDtypeStruct((batch, NSA_KV_HEADS, n_seg, HEAD_DIM), BF16)
    const3 = lambda b, h: (layer, 0, 0, 0)
    return pl.pallas_call(
        functools.partial(_cmp_kernel, seq=seq),
        grid=(batch, NSA_KV_HEADS),
        in_specs=[
            pl.BlockSpec((seq, LANES), lambda b, h: (b, BLK_NKV + h)),
            pl.BlockSpec((seq, LANES), lambda b, h: (b, BLK_NKV + NSA_KV_HEADS + h)),
            pl.BlockSpec((None, 2, L_CMP, HEAD_DIM), const3),
            pl.BlockSpec((None, 2, L_CMP * HEAD_DIM, HEAD_DIM), const3),
            pl.BlockSpec((None, 2, HEAD_DIM, HEAD_DIM), const3),
            pl.BlockSpec((1, HEAD_DIM), lambda b, h: (0, 0)),
        ],
        out_specs=[pl.BlockSpec((None, None, n_seg, HEAD_DIM), lambda b, h: (b, h, 0, 0))] * 2,
        out_shape=[shp, shp],
        scratch_shapes=[pltpu.VMEM((2, seq, HEAD_DIM), F32)],
        compiler_params=pltpu.CompilerParams(
            dimension_semantics=("parallel", "parallel"), vmem_limit_bytes=VMEM_LIMIT_BYTES),
        name="nsa_compress",
    )(proj, proj, cmp_pos, w1, w2, kn0)


def _nsa_kernel(q_ref, kc_ref, vc_ref, ks_ref, vs_ref, kw_ref, vw_ref, small_ref, qn_ref, kn_ref,
                bt_ref, cv_ref, band_ref, ovt_ref, o_ref,
                ksn_ref, vst_ref, kwn_ref, vwt_ref, am_ref, st_ref, vct_ref, *, seq):
    hkv = pl.program_id(1)
    qt = pl.program_id(2)
    TQ, TK, G = NSA_TQ, NSA_TK, NSA_GROUP
    LQ = G * TQ
    n_kt = seq // TK
    blocks_per_tile = TK // L_SLC

    @pl.when(qt == 0)
    def _():
        for kt in range(n_kt):
            rs = slice(kt * TK, (kt + 1) * TK)
            for src, dst, gi in ((ks_ref, ksn_ref, 0), (kw_ref, kwn_ref, 1)):
                x = src[rs, :].astype(F32)
                xn = x * lax.rsqrt(jnp.mean(x * x, axis=-1, keepdims=True) + RMS_EPS) * kn_ref[gi:gi + 1, :]
                dst[rs, :] = xn.astype(BF16)
            vst_ref[kt] = vs_ref[rs, :].astype(F32).T.astype(BF16)
            vwt_ref[kt] = vw_ref[rs, :].astype(F32).T.astype(BF16)
        vct_ref[...] = vc_ref[...].astype(F32).T.astype(BF16)

    scale = (HEAD_DIM ** -0.5) * LOG2E
    qs = []
    for g in range(G):
        x = q_ref[:, g * HEAD_DIM:(g + 1) * HEAD_DIM].astype(F32)
        xn = x * lax.rsqrt(jnp.mean(x * x, axis=-1, keepdims=True) + RMS_EPS) * qn_ref[...]
        qs.append((xn * scale).astype(BF16))
    q = jnp.concatenate(qs, axis=0)

    krow = lax.broadcasted_iota(jnp.int32, (TK, LQ), 0)
    qlane = lax.broadcasted_iota(jnp.int32, (TK, LQ), 1) & (TQ - 1)
    far_bias = cv_ref[...]

    def scores(k_ref, kt):
        return _dot_nt(k_ref[pl.ds(pl.multiple_of(kt * TK, TK), TK), :], q)

    def sel_rows(kt, extra):
        parts = []
        for b in range(blocks_per_tile):
            r = am_ref[pl.ds(kt * blocks_per_tile + b, 1), :] + extra
            parts.append(jnp.broadcast_to(r, (L_SLC, LQ)))
        return jnp.concatenate(parts, axis=0)

    def softmax_pv(carry, s, vt):
        m, l, acc = carry
        m_new = jnp.maximum(m, jnp.max(s, axis=0, keepdims=True))
        alpha = jnp.exp2(m - m_new)
        p = jnp.exp2(s - m_new)
        l = alpha * l + jnp.sum(p, axis=0, keepdims=True)
        return m_new, l, alpha * acc + _dot(vt, p.astype(BF16))

    def init():
        return (jnp.full((1, LQ), NEG_BIG, F32), jnp.zeros((1, LQ), F32), jnp.zeros((HEAD_DIM, LQ), F32))

    kt1 = jnp.maximum(qt - 1, 0)
    kt2 = jnp.maximum(qt - 2, 0)
    pen1 = jnp.where(qt >= 1, 0.0, NEG_BIG)
    pen2 = jnp.where(qt >= 2, 0.0, NEG_BIG)
    causal = krow <= qlane
    n_cmp = (seq - L_CMP) // D_CMP + 1
    n_sel = seq // L_SLC
    band0 = pl.multiple_of(_band_zero(seq) - qt * (TQ // D_CMP), TQ // D_CMP)

    s_c = _dot_nt(kc_ref[...], q) + band_ref[pl.ds(band0, N_CMP_PAD), :]
    s_w0 = scores(kwn_ref, qt)

    nrow = lax.broadcasted_iota(jnp.int32, (N_CMP_PAD, LQ), 0)
    t_lane = qt * TQ + (lax.broadcasted_iota(jnp.int32, (N_CMP_PAD, LQ), 1) & (TQ - 1))
    cmp_end = jnp.where(nrow < n_cmp, nrow * D_CMP + (L_CMP - 1), seq)
    valid = t_lane >= cmp_end
    s_c = jnp.where(valid, s_c, NEG_BIG)
    m_c = jnp.max(s_c, axis=0, keepdims=True)
    p_c = jnp.where(valid, jnp.exp2(s_c - m_c), 0.0)
    den = jnp.sum(p_c, axis=0, keepdims=True)
    p_c = p_c / jnp.where(den > 0, den, 1.0)
    s_w1 = scores(kwn_ref, kt1)

    c_win = softmax_pv(init(), jnp.where(causal, s_w0 + bt_ref[0], NEG_BIG), vwt_ref[qt])
    o_cmp = _dot(vct_ref[...], p_c.astype(BF16))
    p_sum = p_c[:, 0:TQ]
    for g in range(1, G):
        p_sum = p_sum + p_c[:, g * TQ:(g + 1) * TQ]
    imp = jnp.dot(ovt_ref[...], p_sum, precision=lax.Precision.HIGHEST,
                  preferred_element_type=F32)
    s_w2 = scores(kwn_ref, kt2)

    jb = lax.broadcasted_iota(jnp.int32, (n_sel, TQ), 0)
    cur = (qt * TQ + lax.broadcasted_iota(jnp.int32, (n_sel, TQ), 1)) >> int(math.log2(L_SLC))
    val = jnp.where(jb == 0, FORCED_SCORE, jnp.where(jb >= cur - 1, FORCED_SCORE, imp))
    val = jnp.where(jb > cur, -1.0, val)
    cnt = jnp.zeros((n_sel, TQ), F32)
    for i in range(n_sel):
        r = val[i:i + 1, :]
        cnt = cnt + jnp.where(r > val, 1.0, jnp.where(r == val, jnp.where(jb > i, 1.0, 0.0), 0.0))
    add_mask = jnp.where(cnt < min(TOP_N, n_sel), 0.0, NEG_BIG)
    am_ref[...] = jnp.concatenate([add_mask] * G, axis=1)

    c_win = softmax_pv(c_win, s_w1 + bt_ref[1] + pen1, vwt_ref[kt1])
    s_s0 = scores(ksn_ref, qt)
    c_win = softmax_pv(c_win, jnp.where(qlane < krow, s_w2 + (far_bias + pen2), NEG_BIG), vwt_ref[kt2])
    s_s1 = scores(ksn_ref, kt1)
    c_slc = softmax_pv(init(), jnp.where(causal, s_s0 + bt_ref[0] + sel_rows(qt, 0.0), NEG_BIG), vst_ref[qt])
    s_f0 = scores(ksn_ref, 0)
    c_slc = softmax_pv(c_slc, s_s1 + bt_ref[1] + sel_rows(kt1, pen1), vst_ref[kt1])
    c_slc = softmax_pv(c_slc, s_f0 + sel_rows(0, far_bias + pen2), vst_ref[0])

    def slc_far(kt, carry):
        return softmax_pv(carry, scores(ksn_ref, kt) + sel_rows(kt, far_bias), vst_ref[kt])

    m, l, acc = lax.fori_loop(1, jnp.maximum(qt - 1, 1), slc_far, c_slc)
    o_slc = acc / l
    m, l, acc = c_win
    o_win = acc / l

    st_ref[...] = small_ref[...].T

    def gate_row(branch):
        rows = [st_ref[pl.ds(LANE_GATE + branch * NSA_HEADS + hkv * G + g, 1), :] for g in range(G)]
        return jax.nn.sigmoid(jnp.concatenate(rows, axis=1))

    o = gate_row(0) * o_cmp + gate_row(1) * o_slc + gate_row(2) * o_win
    for g in range(G):
        o_ref[:, g * HEAD_DIM:(g + 1) * HEAD_DIM] = o[:, g * TQ:(g + 1) * TQ].T.astype(o_ref.dtype)


def _nsa(proj, small, kc, vc, qn, kn12, bias_tiles, far_vec, band, ovt, batch, seq):
    T = proj.shape[0]
    TQ, TK, G = NSA_TQ, NSA_TK, NSA_GROUP
    nqt = seq // TQ
    n_kt = seq // TK
    gw = G * HEAD_DIM
    LQ = G * TQ
    n_sel = seq // L_SLC
    kvcol = lambda part: (lambda b, h, t: (b, BLK_NKV + part * NSA_KV_HEADS + h))
    return pl.pallas_call(
        functools.partial(_nsa_kernel, seq=seq),
        grid=(batch, NSA_KV_HEADS, nqt),
        in_specs=[
            pl.BlockSpec((TQ, gw), lambda b, h, t: (b * nqt + t, BLK_NQ * LANES // gw + h)),
            pl.BlockSpec((None, None, N_CMP_PAD, HEAD_DIM), lambda b, h, t: (b, h, 0, 0)),
            pl.BlockSpec((None, None, N_CMP_PAD, HEAD_DIM), lambda b, h, t: (b, h, 0, 0)),
            pl.BlockSpec((seq, LANES), kvcol(2)),
            pl.BlockSpec((seq, LANES), kvcol(3)),
            pl.BlockSpec((seq, LANES), kvcol(4)),
            pl.BlockSpec((seq, LANES), kvcol(5)),
            pl.BlockSpec((TQ, LANES), lambda b, h, t: (b * nqt + t, 0)),
            pl.BlockSpec((1, HEAD_DIM), lambda b, h, t: (0, 0)),
            pl.BlockSpec((2, HEAD_DIM), lambda b, h, t: (0, 0)),
            pl.BlockSpec((None, 2, TK, LQ), lambda b, h, t: (h, 0, 0, 0)),
            pl.BlockSpec((None, 1, LQ), lambda b, h, t: (h, 0, 0)),
            pl.BlockSpec((None, _band_zero(seq) + N_CMP_PAD, LQ), lambda b, h, t: (h, 0, 0)),
            pl.BlockSpec((n_sel, N_CMP_PAD), lambda b, h, t: (0, 0)),
        ],
        out_specs=pl.BlockSpec((TQ, gw), lambda b, h, t: (b * nqt + t, h)),
        out_shape=jax.ShapeDtypeStruct((T, NSA_DIM), BF16),
        scratch_shapes=[
            pltpu.VMEM((seq, HEAD_DIM), BF16),
            pltpu.VMEM((n_kt, HEAD_DIM, TK), BF16),
            pltpu.VMEM((seq, HEAD_DIM), BF16),
            pltpu.VMEM((n_kt, HEAD_DIM, TK), BF16),
            pltpu.VMEM((n_sel, LQ), F32),
            pltpu.VMEM((LANES, TQ), F32),
            pltpu.VMEM((HEAD_DIM, N_CMP_PAD), BF16),
        ],
        compiler_params=pltpu.CompilerParams(
            dimension_semantics=("parallel", "parallel", "arbitrary"), vmem_limit_bytes=VMEM_LIMIT_BYTES),
        name="nsa_attention",
    )(proj, kc, vc, proj, proj, proj, proj, small, qn, kn12, bias_tiles, far_vec, band, ovt)


def _outproj_kernel(x_ref, yg_ref, yn_ref, cu_ref, cb_ref, cc_ref, cup_ref, ccp_ref, sw_ref,
                    wg_ref, wn_ref, wc_ref, o_ref, *, tiles_per_seq):
    i = pl.program_id(0)
    v = cc_ref[...].astype(F32) * cu_ref[...].astype(F32)
    vp = (ccp_ref[...].astype(F32) * cup_ref[...].astype(F32))[BF16_ROWS - SUBLANES:, :]
    vp = jnp.where(i % tiles_per_seq == 0, 0.0, vp)
    sw = sw_ref[...]
    y = v * sw[2:3, :]
    for s in range(1, SHORT_CONV_W):
        y = y + _shift_rows(v, vp, s) * sw[2 - s:3 - s, :]
    yc = (cb_ref[...].astype(F32) * y).astype(BF16)
    o_ref[...] = (x_ref[...] + _dot(yg_ref[...], wg_ref[...]) + _dot(yn_ref[...], wn_ref[...])
                  + _dot(yc, wc_ref[...]))


def _outproj(x2, y_gdn, y_nsa, proj, sconv_t, w_out, layer, seq, tm=512):
    T, D = x2.shape
    cb = CONV_DIM // LANES
    prev = lambda blk: (lambda i: (jnp.maximum(i * (tm // BF16_ROWS) - 1, 0), blk // cb))
    return pl.pallas_call(
        functools.partial(_outproj_kernel, tiles_per_seq=seq // tm),
        grid=(T // tm,),
        in_specs=[
            pl.BlockSpec((tm, D), lambda i: (i, 0)),
            pl.BlockSpec((tm, GDN_DIM), lambda i: (i, 0)),
            pl.BlockSpec((tm, NSA_DIM), lambda i: (i, 0)),
            pl.BlockSpec((tm, CONV_DIM), lambda i: (i, BLK_CU // cb)),
            pl.BlockSpec((tm, CONV_DIM), lambda i: (i, BLK_CB // cb)),
            pl.BlockSpec((tm, CONV_DIM), lambda i: (i, BLK_CC // cb)),
            pl.BlockSpec((BF16_ROWS, CONV_DIM), prev(BLK_CU)),
            pl.BlockSpec((BF16_ROWS, CONV_DIM), prev(BLK_CC)),
            pl.BlockSpec((None, SHORT_CONV_W, CONV_DIM), lambda i: (layer, 0, 0)),
            pl.BlockSpec((None, GDN_DIM, D), lambda i: (layer, 0, 0)),
            pl.BlockSpec((None, NSA_DIM, D), lambda i: (layer, GDN_DIM // NSA_DIM, 0)),
            pl.BlockSpec((None, CONV_DIM, D), lambda i: (layer, (GDN_DIM + NSA_DIM) // CONV_DIM, 0)),
        ],
        out_specs=pl.BlockSpec((tm, D), lambda i: (i, 0)),
        out_shape=jax.ShapeDtypeStruct((T, D), F32),
        compiler_params=pltpu.CompilerParams(
            dimension_semantics=("parallel",), vmem_limit_bytes=VMEM_LIMIT_BYTES),
        name="outproj",
    )(x2, y_gdn, y_nsa, proj, proj, proj, proj, proj, sconv_t, w_out, w_out, w_out)


def _ffn_kernel(x_ref, g_ref, wg_ref, wu_ref, wd_ref, o_ref, h_ref):
    @pl.when(pl.program_id(1) == 0)
    def _():
        x = x_ref[...]
        ms = jnp.mean(x * x, axis=-1, keepdims=True)
        h_ref[...] = (x * lax.rsqrt(ms + RMS_EPS) * g_ref[...]).astype(BF16)
        o_ref[...] = x

    h = h_ref[...]
    a = _dot(h, wg_ref[...])
    b = _dot(h, wu_ref[...])
    o_ref[...] += _dot((_silu(a) * b).astype(BF16), wd_ref[...])


def _ffn(x2, g, w_gate, w_up, w_down, layer, tm=1024, tf=256):
    T, D = x2.shape
    F = w_gate.shape[2]
    return pl.pallas_call(
        _ffn_kernel,
        grid=(T // tm, F // tf),
        in_specs=[
            pl.BlockSpec((tm, D), lambda i, j: (i, 0)),
            pl.BlockSpec((1, D), lambda i, j: (0, 0)),
            pl.BlockSpec((None, D, tf), lambda i, j: (layer, 0, j)),
            pl.BlockSpec((None, D, tf), lambda i, j: (layer, 0, j)),
            pl.BlockSpec((None, tf, D), lambda i, j: (layer, j, 0)),
        ],
        out_specs=pl.BlockSpec((tm, D), lambda i, j: (i, 0)),
        out_shape=jax.ShapeDtypeStruct((T, D), F32),
        scratch_shapes=[pltpu.VMEM((tm, D), BF16)],
        compiler_params=pltpu.CompilerParams(
            dimension_semantics=("parallel", "arbitrary"), vmem_limit_bytes=VMEM_LIMIT_BYTES),
        name="ffn",
    )(x2, g, w_gate, w_up, w_down)


def _t5_bucket_np(dist):
    n = np.maximum(dist, 0)
    max_exact = NUM_BUCKETS // 2
    nf = np.maximum(n, 1).astype(np.float32)
    large = max_exact + (np.log(nf / np.float32(max_exact)) / np.float32(math.log(MAX_DISTANCE / max_exact))
                         * np.float32(NUM_BUCKETS - max_exact)).astype(np.int32)
    large = np.minimum(large, NUM_BUCKETS - 1)
    return np.where(n < max_exact, n, large).astype(np.int32)


def _bias_lookup_kernel(rb_ref, idx_ref, o_ref):
    idx = idx_ref[...]
    tq = idx.shape[1]
    for h in range(NSA_HEADS):
        acc = jnp.zeros(idx.shape, F32)
        for b in range(NUM_BUCKETS):
            acc = jnp.where(idx == b, rb_ref[b * NSA_HEADS + h] * LOG2E, acc)
        g = h % NSA_GROUP
        o_ref[h // NSA_GROUP, :, g * tq:(g + 1) * tq] = acc


def _bias_lookup(rel_bias, idx):
    rows, tq = idx.shape
    return pl.pallas_call(
        _bias_lookup_kernel,
        in_specs=[pl.BlockSpec(memory_space=pltpu.SMEM),
                  pl.BlockSpec((rows, tq), lambda: (0, 0))],
        out_specs=pl.BlockSpec((NSA_KV_HEADS, rows, NSA_GROUP * tq), lambda: (0, 0, 0)),
        out_shape=jax.ShapeDtypeStruct((NSA_KV_HEADS, rows, NSA_GROUP * tq), F32),
        compiler_params=pltpu.CompilerParams(vmem_limit_bytes=VMEM_LIMIT_BYTES),
        name="bias_lookup",
    )(rel_bias.astype(F32).reshape(-1), jnp.asarray(idx))


def _nsa_tables(rel_bias, seq):
    TQ, TK, G = NSA_TQ, NSA_TK, NSA_GROUP
    j = np.arange(TK)[:, None]
    i = np.arange(TQ)[None, :]
    r = np.arange(_band_zero(seq) + N_CMP_PAD)[:, None]
    buckets = [_t5_bucket_np(0 * TK + i - j), _t5_bucket_np(1 * TK + i - j),
               np.full((SUBLANES, TQ), NUM_BUCKETS - 1, np.int32),
               _t5_bucket_np(i - (D_CMP * (r - _band_zero(seq)) + L_CMP - 1))]
    table = _bias_lookup(rel_bias, np.concatenate(buckets, axis=0).astype(np.int32))
    tiles = table[:, :2 * TK].reshape(NSA_KV_HEADS, 2, TK, G * TQ)
    far = table[:, 2 * TK:2 * TK + 1]
    band = table[:, 2 * TK + SUBLANES:]
    n_cmp = (seq - L_CMP) // D_CMP + 1
    n_sel = seq // L_SLC
    cmp_start = np.arange(n_cmp) * D_CMP
    sel_start = np.arange(n_sel) * L_SLC
    overlap = ((cmp_start[:, None] < sel_start[None, :] + L_SLC)
               & (cmp_start[:, None] + L_CMP > sel_start[None, :])).astype(np.float32)
    ovt = np.zeros((n_sel, N_CMP_PAD), np.float32)
    ovt[:, :n_cmp] = overlap.T
    return tiles, far, band, jnp.asarray(ovt)


def _permute_w_in(w_in):
    L, D, _ = w_in.shape
    w_in = w_in.astype(BF16)
    o = np.cumsum([0, 3 * GDN_DIM, GDN_DIM, GDN_HEADS, GDN_HEADS, NSA_DIM, 6 * NSA_KV_HEADS * HEAD_DIM,
                   3 * NSA_HEADS, CONV_DIM, CONV_DIM, CONV_DIM])
    seg = lambda a, b: w_in[:, :, o[a]:o[b]]
    zeros = lambda n: jnp.zeros((L, D, n), w_in.dtype)
    n_small = 2 * GDN_HEADS + 3 * NSA_HEADS
    parts = [seg(0, 2),
             seg(4, 6),
             seg(2, 4), seg(6, 7), zeros(LANES - n_small),
             zeros(LANES),
             seg(7, 10)]
    w = jnp.concatenate(parts, axis=-1)
    assert w.shape[-1] == PROJ_PAD
    return w


def kernel(x, rel_bias, norm_mix, w_in, gdn_conv, gdn_a_log, gdn_dt_bias, gdn_norm, nsa_q_norm, nsa_k_norm,
           cmp_pos, cmp_w1, cmp_w2, sconv_w, w_out, norm_ffn, w_gate, w_up, w_down):
    B, S, D = x.shape
    depth = w_in.shape[0]
    T = B * S
    x2 = x.reshape(T, D)

    w_in_p = _permute_w_in(w_in)
    w_out_b = w_out.astype(BF16)
    w_gate_b = w_gate.astype(BF16)
    w_up_b = w_up.astype(BF16)
    w_down_b = w_down.astype(BF16)
    cmp_w1_b = cmp_w1.astype(BF16)
    cmp_w2_b = cmp_w2.astype(BF16)
    conv_t = jnp.transpose(gdn_conv, (0, 2, 1))
    sconv_t = jnp.transpose(sconv_w, (0, 2, 1))
    pad_a = lambda v: jnp.pad(v, ((0, 0), (LANE_A, LANES - LANE_A - GDN_HEADS)))[:, None, :]
    alog_vec = pad_a(gdn_a_log)
    dt_vec = pad_a(gdn_dt_bias)
    bias_tiles, far_vec, band, ovt = _nsa_tables(rel_bias, S)

    for l in range(depth):
        proj, small = _inproj(x2, norm_mix[l][None, :], w_in_p, l)
        y_gdn = _gdn(proj, small, conv_t, alog_vec[l], dt_vec[l], gdn_norm[l][None, :], l, B, S)
        kc, vc = _compress(proj, cmp_pos, cmp_w1_b, cmp_w2_b, nsa_k_norm[l, 0][None, :], l, B, S)
        y_nsa = _nsa(proj, small, kc, vc, nsa_q_norm[l][None, :], nsa_k_norm[l, 1:3], bias_tiles, far_vec,
                     band, ovt, B, S)
        x2 = _outproj(x2, y_gdn, y_nsa, proj, sconv_t, w_out_b, l, S)
        x2 = _ffn(x2, norm_ffn[l][None, :], w_gate_b, w_up_b, w_down_b, l)
    return x2.reshape(B, S, D)
```

```python
import functools
import math

import numpy as np
import jax
import jax.numpy as jnp
from jax import lax
from jax.experimental import pallas as pl
from jax.experimental.pallas import tpu as pltpu

HEAD_DIM = 128
GDN_HEADS = 6
NSA_HEADS = 6
NSA_KV_HEADS = 2
NSA_GROUP = NSA_HEADS // NSA_KV_HEADS
CONV_DIM = 512
GDN_DIM = GDN_HEADS * HEAD_DIM
NSA_DIM = NSA_HEADS * HEAD_DIM
GDN_CONV_W = 4
GDN_CHUNK = 64
L_CMP = 32
D_CMP = 16
L_SLC = 64
TOP_N = 8
WINDOW = 512
NUM_BUCKETS = 32
MAX_DISTANCE = 128
SHORT_CONV_W = 3
RMS_EPS = 1e-6
FORCED_SCORE = 1e4
NEG_BIG = -1e30
LOG2E = math.log2(math.e)

LANES = 128
SUBLANES = 8
BF16_ROWS = 16
VMEM_LIMIT_BYTES = 56 * 1024 * 1024

BLK_QKV = 0
BLK_Z = 18
BLK_NQ = 24
BLK_NKV = 30
BLK_SMALL = 42
BLK_CU = 44
BLK_CB = 48
BLK_CC = 52
N_BLK = 56
PROJ_PAD = N_BLK * LANES
LANE_BETA = 0
LANE_A = GDN_HEADS
LANE_GATE = 2 * GDN_HEADS

GDN_GROUP_ROWS = 256
GDN_HEADS_PER_PROGRAM = 6
N_CMP_PAD = 128
NSA_TQ = 256
NSA_TK = 256

F32 = jnp.float32
BF16 = jnp.bfloat16


def _band_zero(seq):
    return (seq // NSA_TQ - 1) * (NSA_TQ // D_CMP)


def _dot(a, b):
    return jnp.dot(a, b, preferred_element_type=F32)


def _dot_nt(a, b):
    return lax.dot_general(a, b, (((1,), (1,)), ((), ())), preferred_element_type=F32)


def _dot_tn(a, b):
    return lax.dot_general(a, b, (((0,), (0,)), ((), ())), preferred_element_type=F32)


def _silu(x):
    hx = 0.5 * x
    return hx + hx * jnp.tanh(hx)


def _softplus(x):
    return jnp.maximum(x, 0.0) + jnp.log1p(jnp.exp(-jnp.abs(x)))


def _lane_pick(x, lane):
    idx = lax.broadcasted_iota(jnp.int32, x.shape, 1)
    return jnp.sum(jnp.where(idx == lane, x, 0.0), axis=-1, keepdims=True)


def _round_robin(gens):
    results = [None] * len(gens)
    live = list(range(len(gens)))
    while live:
        for i in list(live):
            try:
                next(gens[i])
            except StopIteration as stop:
                results[i] = stop.value
                live.remove(i)
    return tuple(results)


def _shift_rows(cur, prev, s):
    rolled = pltpu.roll(cur, s, 0)
    rows = lax.broadcasted_iota(jnp.int32, cur.shape, 0)
    out = rolled
    for r in range(s):
        out = jnp.where(rows == r, prev[SUBLANES - s + r:SUBLANES - s + r + 1, :], out)
    return out


def _inproj_kernel(x_ref, g_ref, w_ref, o_ref, small_ref, h_ref, *, small_tile, small_off):
    j = pl.program_id(1)

    @pl.when(j == 0)
    def _():
        x = x_ref[...]
        ms = jnp.mean(x * x, axis=-1, keepdims=True)
        h_ref[...] = (x * lax.rsqrt(ms + RMS_EPS) * g_ref[...]).astype(BF16)

    acc = _dot(h_ref[...], w_ref[...])
    o_ref[...] = acc.astype(o_ref.dtype)

    @pl.when(j == small_tile)
    def _():
        small_ref[...] = acc[:, small_off:small_off + LANES]


def _inproj(x2, g, w, layer, tm=1024, tn=1792):
    T, D = x2.shape
    N = w.shape[2]
    small_col = BLK_SMALL * LANES
    return pl.pallas_call(
        functools.partial(_inproj_kernel, small_tile=small_col // tn, small_off=small_col % tn),
        grid=(T // tm, N // tn),
        in_specs=[
            pl.BlockSpec((tm, D), lambda i, j: (i, 0)),
            pl.BlockSpec((1, D), lambda i, j: (0, 0)),
            pl.BlockSpec((None, D, tn), lambda i, j: (layer, 0, j)),
        ],
        out_specs=[pl.BlockSpec((tm, tn), lambda i, j: (i, j)),
                   pl.BlockSpec((tm, LANES), lambda i, j: (i, 0))],
        out_shape=[jax.ShapeDtypeStruct((T, N), BF16), jax.ShapeDtypeStruct((T, LANES), F32)],
        scratch_shapes=[pltpu.VMEM((tm, D), BF16)],
        compiler_params=pltpu.CompilerParams(
            dimension_semantics=("parallel", "arbitrary"), vmem_limit_bytes=VMEM_LIMIT_BYTES),
        name="inproj",
    )(x2, g, w)


def _gdn_kernel(q_ref, k_ref, v_ref, z_ref, small_ref, cq_ref, ck_ref, cv_ref, alog_ref, dt_ref,
                ng_ref, o_ref, conv_buf, *, seq):
    R = GDN_GROUP_ROWS
    C = GDN_CHUNK
    n_groups = seq // R
    rows = lax.broadcasted_iota(jnp.int32, (R, R), 0)
    cols = lax.broadcasted_iota(jnp.int32, (R, R), 1)
    same = (rows & -C) == (cols & -C)
    tril = same & (rows >= cols)
    strict = same & (rows > cols)
    row_in_chunk = lax.broadcasted_iota(jnp.int32, (R, LANES), 0) & (C - 1)
    neg_a = -jnp.exp(alog_ref[...])

    def conv_silu(ref, w_ref, buf, ls, r0, first):
        cur = ref[pl.ds(r0, R), ls].astype(F32)
        p0 = pl.multiple_of(jnp.maximum(r0 - BF16_ROWS, 0), BF16_ROWS)
        prev = ref[pl.ds(p0, BF16_ROWS), ls].astype(F32)[BF16_ROWS - SUBLANES:, :]
        buf[0:SUBLANES, :] = jnp.where(first, 0.0, prev)
        buf[SUBLANES:SUBLANES + R, :] = cur
        w = w_ref[:, ls]
        y = cur * w[3:4, :]
        for s in range(1, GDN_CONV_W):
            y = y + buf[SUBLANES - s:SUBLANES - s + R, :] * w[3 - s:4 - s, :]
        return _silu(y)

    def group(gi, states):
        small = small_ref[pl.ds(pl.multiple_of(gi * R, R), R), :]
        beta_all = jax.nn.sigmoid(small)
        gc_all = neg_a * _softplus(small + dt_ref[...])
        s = 1
        while s < C:
            gc_all = gc_all + jnp.where(row_in_chunk >= s, pltpu.roll(gc_all, s, 0), 0.0)
            s *= 2
        return _round_robin([head_group(gi, states[hh], hh, beta_all, gc_all)
                             for hh in range(GDN_HEADS_PER_PROGRAM)])

    def head_group(gi, state, hh, beta_all, gc_all):
        h = pl.program_id(1) * GDN_HEADS_PER_PROGRAM + hh
        ls = slice(hh * HEAD_DIM, (hh + 1) * HEAD_DIM)
        r0 = pl.multiple_of(gi * R, R)
        first = gi == 0
        q = conv_silu(q_ref, cq_ref, conv_buf.at[3 * hh], ls, r0, first)
        k = conv_silu(k_ref, ck_ref, conv_buf.at[3 * hh + 1], ls, r0, first)
        v = conv_silu(v_ref, cv_ref, conv_buf.at[3 * hh + 2], ls, r0, first)
        q = q * lax.rsqrt(jnp.sum(q * q, axis=-1, keepdims=True) + RMS_EPS) * (HEAD_DIM ** -0.5)
        k = k * lax.rsqrt(jnp.sum(k * k, axis=-1, keepdims=True) + RMS_EPS)
        beta = _lane_pick(beta_all, LANE_BETA + h)
        gc = jnp.broadcast_to(_lane_pick(gc_all, LANE_A + h), (R, LANES))
        gc_col = jnp.concatenate([gc] * (R // LANES), axis=1)
        gc_row = gc_col.T
        decay = jnp.where(tril, jnp.exp(jnp.where(tril, gc_col - gc_row, 0.0)), 0.0)
        kb = k * beta
        vb = v * beta
        k16 = k.astype(BF16)
        yield
        lmat = jnp.where(strict, _dot_nt(kb.astype(BF16), k16) * decay, 0.0)
        attn = jnp.where(tril, _dot_nt(q.astype(BF16), k16) * decay, 0.0)
        yield
        y = -lmat
        p16 = lmat.astype(BF16)
        n = 1
        while 2 * n < C:
            p = _dot(p16, p16)
            yield
            n *= 2
            p16 = p.astype(BF16)
            y = y + p + _dot(y.astype(BF16), p16)
            yield
        eg = jnp.exp(gc)
        rhs = jnp.concatenate([vb, kb * eg], axis=1)
        sol = rhs + _dot(y.astype(BF16), rhs.astype(BF16))
        yield
        u = sol[:, :HEAD_DIM]
        w = sol[:, HEAD_DIM:]
        qg = (q * eg).astype(BF16)
        outs = []
        for c in range(R // C):
            sl = slice(c * C, (c + 1) * C)
            g_last = gc[c * C + C - 1:c * C + C, :]
            kdec = k[sl] * jnp.exp(g_last - gc[sl])
            s16 = state.astype(BF16)
            v_new = u[sl] - _dot(w[sl].astype(BF16), s16)
            yield
            vn16 = v_new.astype(BF16)
            o_c = _dot(qg[sl], s16) + _dot(attn[sl, sl].astype(BF16), vn16)
            state = state * jnp.exp(g_last[:, 0:1]) + _dot_tn(kdec.astype(BF16), vn16)
            yield
            outs.append(o_c)
        o = jnp.concatenate(outs, axis=0)
        o = o * lax.rsqrt(jnp.mean(o * o, axis=-1, keepdims=True) + RMS_EPS) * ng_ref[...]
        o_ref[pl.ds(r0, R), ls] = (o * _silu(z_ref[pl.ds(r0, R), ls].astype(F32))).astype(o_ref.dtype)
        return state

    lax.fori_loop(0, n_groups, group,
                  tuple(jnp.zeros((HEAD_DIM, HEAD_DIM), F32) for _ in range(GDN_HEADS_PER_PROGRAM)))


def _gdn(proj, small, conv_t, alog_vec, dt_vec, norm_g, layer, batch, seq):
    T = proj.shape[0]
    P = GDN_HEADS_PER_PROGRAM
    W = P * LANES
    col = lambda blk: (lambda b, h: (b, blk // P + h))
    wcol = lambda blk: (lambda b, h: (layer, 0, blk // P + h))
    const = lambda b, h: (0, 0)
    return pl.pallas_call(
        functools.partial(_gdn_kernel, seq=seq),
        grid=(batch, GDN_HEADS // P),
        in_specs=[
            pl.BlockSpec((seq, W), col(BLK_QKV)),
            pl.BlockSpec((seq, W), col(BLK_QKV + GDN_HEADS)),
            pl.BlockSpec((seq, W), col(BLK_QKV + 2 * GDN_HEADS)),
            pl.BlockSpec((seq, W), col(BLK_Z)),
            pl.BlockSpec((seq, LANES), lambda b, h: (b, 0)),
            pl.BlockSpec((None, GDN_CONV_W, W), wcol(0)),
            pl.BlockSpec((None, GDN_CONV_W, W), wcol(GDN_HEADS)),
            pl.BlockSpec((None, GDN_CONV_W, W), wcol(2 * GDN_HEADS)),
            pl.BlockSpec((1, LANES), const),
            pl.BlockSpec((1, LANES), const),
            pl.BlockSpec((1, LANES), const),
        ],
        out_specs=pl.BlockSpec((seq, W), lambda b, h: (b, h)),
        out_shape=jax.ShapeDtypeStruct((T, GDN_DIM), BF16),
        scratch_shapes=[pltpu.VMEM((3 * P, SUBLANES + GDN_GROUP_ROWS, HEAD_DIM), F32)],
        compiler_params=pltpu.CompilerParams(
            dimension_semantics=("parallel", "parallel"), vmem_limit_bytes=VMEM_LIMIT_BYTES),
        name="gdn",
    )(proj, proj, proj, proj, small, conv_t, conv_t, conv_t, alog_vec, dt_vec, norm_g)


def _cmp_kernel(kt_ref, vt_ref, pos_ref, w1_ref, w2_ref, kn_ref, kc_ref, vc_ref, tok_f32, *, seq):
    n_seg = seq // D_CMP
    out_refs = (kc_ref, vc_ref)
    for which, tok_ref in enumerate((kt_ref, vt_ref)):
        tok_f32[which] = tok_ref[...].astype(F32)
        hi = jnp.zeros((n_seg, HEAD_DIM), F32)
        lo = jnp.zeros((n_seg, HEAD_DIM), F32)
        for p in range(D_CMP):
            seg = tok_f32[which, pl.ds(p, n_seg, stride=D_CMP), :]
            a = (seg + pos_ref[which, p:p + 1, :]).astype(BF16)
            hi = hi + _dot(a, w1_ref[which, p * HEAD_DIM:(p + 1) * HEAD_DIM, :])
            b = (seg + pos_ref[which, D_CMP + p:D_CMP + p + 1, :]).astype(BF16)
            lo = lo + _dot(b, w1_ref[which, (D_CMP + p) * HEAD_DIM:(D_CMP + p + 1) * HEAD_DIM, :])
        hid = hi + pltpu.roll(lo, n_seg - 1, 0)
        out = _dot(_silu(hid).astype(BF16), w2_ref[which])
        if which == 0:
            out = out * lax.rsqrt(jnp.mean(out * out, axis=-1, keepdims=True) + RMS_EPS) * kn_ref[...]
        out_refs[which][...] = out.astype(BF16)


def _compress(proj, cmp_pos, w1, w2, kn0, layer, batch, seq):
    n_seg = seq // D_CMP
    shp = jax.ShapeDtypeStruct((batch, NSA_KV_HEADS, n_seg, HEAD_DIM), BF16)
    const3 = lambda b, h: (layer, 0, 0, 0)
    return pl.pallas_call(
        functools.partial(_cmp_kernel, seq=seq),
        grid=(batch, NSA_KV_HEADS),
        in_specs=[
            pl.BlockSpec((seq, LANES), lambda b, h: (b, BLK_NKV + h)),
            pl.BlockSpec((seq, LANES), lambda b, h: (b, BLK_NKV + NSA_KV_HEADS + h)),
            pl.BlockSpec((None, 2, L_CMP, HEAD_DIM), const3),
            pl.BlockSpec((None, 2, L_CMP * HEAD_DIM, HEAD_DIM), const3),
            pl.BlockSpec((None, 2, HEAD_DIM, HEAD_DIM), const3),
            pl.BlockSpec((1, HEAD_DIM), lambda b, h: (0, 0)),
        ],
        out_specs=[pl.BlockSpec((None, None, n_seg, HEAD_DIM), lambda b, h: (b, h, 0, 0))] * 2,
        out_shape=[shp, shp],
        scratch_shapes=[pltpu.VMEM((2, seq, HEAD_DIM), F32)],
        compiler_params=pltpu.CompilerParams(
            dimension_semantics=("parallel", "parallel"), vmem_limit_bytes=VMEM_LIMIT_BYTES),
        name="nsa_compress",
    )(proj, proj, cmp_pos, w1, w2, kn0)


def _nsa_kernel(q_ref, kc_ref, vc_ref, ks_ref, vs_ref, kw_ref, vw_ref, small_ref, qn_ref, kn_ref,
                bt_ref, cv_ref, band_ref, ovt_ref, o_ref,
                ksn_ref, vst_ref, kwn_ref, vwt_ref, am_ref, st_ref, vct_ref, *, seq):
    hkv = pl.program_id(1)
    qt = pl.program_id(2)
    TQ, TK, G = NSA_TQ, NSA_TK, NSA_GROUP
    LQ = G * TQ
    n_kt = seq // TK
    blocks_per_tile = TK // L_SLC

    @pl.when(qt == 0)
    def _():
        for kt in range(n_kt):
            rs = slice(kt * TK, (kt + 1) * TK)
            for src, dst, gi in ((ks_ref, ksn_ref, 0), (kw_ref, kwn_ref, 1)):
                x = src[rs, :].astype(F32)
                xn = x * lax.rsqrt(jnp.mean(x * x, axis=-1, keepdims=True) + RMS_EPS) * kn_ref[gi:gi + 1, :]
                dst[rs, :] = xn.astype(BF16)
            vst_ref[kt] = vs_ref[rs, :].astype(F32).T.astype(BF16)
            vwt_ref[kt] = vw_ref[rs, :].astype(F32).T.astype(BF16)
        vct_ref[...] = vc_ref[...].astype(F32).T.astype(BF16)

    scale = (HEAD_DIM ** -0.5) * LOG2E
    qs = []
    for g in range(G):
        x = q_ref[:, g * HEAD_DIM:(g + 1) * HEAD_DIM].astype(F32)
        xn = x * lax.rsqrt(jnp.mean(x * x, axis=-1, keepdims=True) + RMS_EPS) * qn_ref[...]
        qs.append((xn * scale).astype(BF16))
    q = jnp.concatenate(qs, axis=0)

    krow = lax.broadcasted_iota(jnp.int32, (TK, LQ), 0)
    qlane = lax.broadcasted_iota(jnp.int32, (TK, LQ), 1) & (TQ - 1)
    far_bias = cv_ref[...]

    def scores(k_ref, kt):
        return _dot_nt(k_ref[pl.ds(pl.multiple_of(kt * TK, TK), TK), :], q)

    def sel_rows(kt, extra):
        parts = []
        for b in range(blocks_per_tile):
            r = am_ref[pl.ds(kt * blocks_per_tile + b, 1), :] + extra
            parts.append(jnp.broadcast_to(r, (L_SLC, LQ)))
        return jnp.concatenate(parts, axis=0)

    def softmax_pv(carry, s, vt):
        m, l, acc = carry
        m_new = jnp.maximum(m, jnp.max(s, axis=0, keepdims=True))
        alpha = jnp.exp2(m - m_new)
        p = jnp.exp2(s - m_new)
        l = alpha * l + jnp.sum(p, axis=0, keepdims=True)
        return m_new, l, alpha * acc + _dot(vt, p.astype(BF16))

    def init():
        return (jnp.full((1, LQ), NEG_BIG, F32), jnp.zeros((1, LQ), F32), jnp.zeros((HEAD_DIM, LQ), F32))

    kt1 = jnp.maximum(qt - 1, 0)
    kt2 = jnp.maximum(qt - 2, 0)
    pen1 = jnp.where(qt >= 1, 0.0, NEG_BIG)
    pen2 = jnp.where(qt >= 2, 0.0, NEG_BIG)
    causal = krow <= qlane
    n_cmp = (seq - L_CMP) // D_CMP + 1
    n_sel = seq // L_SLC
    band0 = pl.multiple_of(_band_zero(seq) - qt * (TQ // D_CMP), TQ // D_CMP)

    s_c = _dot_nt(kc_ref[...], q) + band_ref[pl.ds(band0, N_CMP_PAD), :]
    s_w0 = scores(kwn_ref, qt)

    nrow = lax.broadcasted_iota(jnp.int32, (N_CMP_PAD, LQ), 0)
    t_lane = qt * TQ + (lax.broadcasted_iota(jnp.int32, (N_CMP_PAD, LQ), 1) & (TQ - 1))
    cmp_end = jnp.where(nrow < n_cmp, nrow * D_CMP + (L_CMP - 1), seq)
    valid = t_lane >= cmp_end
    s_c = jnp.where(valid, s_c, NEG_BIG)
    m_c = jnp.max(s_c, axis=0, keepdims=True)
    p_c = jnp.where(valid, jnp.exp2(s_c - m_c), 0.0)
    den = jnp.sum(p_c, axis=0, keepdims=True)
    p_c = p_c / jnp.where(den > 0, den, 1.0)
    s_w1 = scores(kwn_ref, kt1)

    c_win = softmax_pv(init(), jnp.where(causal, s_w0 + bt_ref[0], NEG_BIG), vwt_ref[qt])
    o_cmp = _dot(vct_ref[...], p_c.astype(BF16))
    p_sum = p_c[:, 0:TQ]
    for g in range(1, G):
        p_sum = p_sum + p_c[:, g * TQ:(g + 1) * TQ]
    imp = jnp.dot(ovt_ref[...], p_sum, precision=lax.Precision.HIGHEST,
                  preferred_element_type=F32)
    s_w2 = scores(kwn_ref, kt2)

    jb = lax.broadcasted_iota(jnp.int32, (n_sel, TQ), 0)
    cur = (qt * TQ + lax.broadcasted_iota(jnp.int32, (n_sel, TQ), 1)) >> int(math.log2(L_SLC))
    val = jnp.where(jb == 0, FORCED_SCORE, jnp.where(jb >= cur - 1, FORCED_SCORE, imp))
    val = jnp.where(jb > cur, -1.0, val)
    cnt = jnp.zeros((n_sel, TQ), F32)
    for i in range(n_sel):
        r = val[i:i + 1, :]
        cnt = cnt + jnp.where(r > val, 1.0, jnp.where(r == val, jnp.where(jb > i, 1.0, 0.0), 0.0))
    add_mask = jnp.where(cnt < min(TOP_N, n_sel), 0.0, NEG_BIG)
    am_ref[...] = jnp.concatenate([add_mask] * G, axis=1)

    c_win = softmax_pv(c_win, s_w1 + bt_ref[1] + pen1, vwt_ref[kt1])
    s_s0 = scores(ksn_ref, qt)
    c_win = softmax_pv(c_win, jnp.where(qlane < krow, s_w2 + (far_bias + pen2), NEG_BIG), vwt_ref[kt2])
    s_s1 = scores(ksn_ref, kt1)
    c_slc = softmax_pv(init(), jnp.where(causal, s_s0 + bt_ref[0] + sel_rows(qt, 0.0), NEG_BIG), vst_ref[qt])
    s_f0 = scores(ksn_ref, 0)
    c_slc = softmax_pv(c_slc, s_s1 + bt_ref[1] + sel_rows(kt1, pen1), vst_ref[kt1])
    c_slc = softmax_pv(c_slc, s_f0 + sel_rows(0, far_bias + pen2), vst_ref[0])

    def slc_far(kt, carry):
        return softmax_pv(carry, scores(ksn_ref, kt) + sel_rows(kt, far_bias), vst_ref[kt])

    m, l, acc = lax.fori_loop(1, jnp.maximum(qt - 1, 1), slc_far, c_slc)
    o_slc = acc / l
    m, l, acc = c_win
    o_win = acc / l

    st_ref[...] = small_ref[...].T

    def gate_row(branch):
        rows = [st_ref[pl.ds(LANE_GATE + branch * NSA_HEADS + hkv * G + g, 1), :] for g in range(G)]
        return jax.nn.sigmoid(jnp.concatenate(rows, axis=1))

    o = gate_row(0) * o_cmp + gate_row(1) * o_slc + gate_row(2) * o_win
    for g in range(G):
        o_ref[:, g * HEAD_DIM:(g + 1) * HEAD_DIM] = o[:, g * TQ:(g + 1) * TQ].T.astype(o_ref.dtype)


def _nsa(proj, small, kc, vc, qn, kn12, bias_tiles, far_vec, band, ovt, batch, seq):
    T = proj.shape[0]
    TQ, TK, G = NSA_TQ, NSA_TK, NSA_GROUP
    nqt = seq // TQ
    n_kt = seq // TK
    gw = G * HEAD_DIM
    LQ = G * TQ
    n_sel = seq // L_SLC
    kvcol = lambda part: (lambda b, h, t: (b, BLK_NKV + part * NSA_KV_HEADS + h))
    return pl.pallas_call(
        functools.partial(_nsa_kernel, seq=seq),
        grid=(batch, NSA_KV_HEADS, nqt),
        in_specs=[
            pl.BlockSpec((TQ, gw), lambda b, h, t: (b * nqt + t, BLK_NQ * LANES // gw + h)),
            pl.BlockSpec((None, None, N_CMP_PAD, HEAD_DIM), lambda b, h, t: (b, h, 0, 0)),
            pl.BlockSpec((None, None, N_CMP_PAD, HEAD_DIM), lambda b, h, t: (b, h, 0, 0)),
            pl.BlockSpec((seq, LANES), kvcol(2)),
            pl.BlockSpec((seq, LANES), kvcol(3)),
            pl.BlockSpec((seq, LANES), kvcol(4)),
            pl.BlockSpec((seq, LANES), kvcol(5)),
            pl.BlockSpec((TQ, LANES), lambda b, h, t: (b * nqt + t, 0)),
            pl.BlockSpec((1, HEAD_DIM), lambda b, h, t: (0, 0)),
            pl.BlockSpec((2, HEAD_DIM), lambda b, h, t: (0, 0)),
            pl.BlockSpec((None, 2, TK, LQ), lambda b, h, t: (h, 0, 0, 0)),
            pl.BlockSpec((None, 1, LQ), lambda b, h, t: (h, 0, 0)),
            pl.BlockSpec((None, _band_zero(seq) + N_CMP_PAD, LQ), lambda b, h, t: (h, 0, 0)),
            pl.BlockSpec((n_sel, N_CMP_PAD), lambda b, h, t: (0, 0)),
        ],
        out_specs=pl.BlockSpec((TQ, gw), lambda b, h, t: (b * nqt + t, h)),
        out_shape=jax.ShapeDtypeStruct((T, NSA_DIM), BF16),
        scratch_shapes=[
            pltpu.VMEM((seq, HEAD_DIM), BF16),
            pltpu.VMEM((n_kt, HEAD_DIM, TK), BF16),
            pltpu.VMEM((seq, HEAD_DIM), BF16),
            pltpu.VMEM((n_kt, HEAD_DIM, TK), BF16),
            pltpu.VMEM((n_sel, LQ), F32),
            pltpu.VMEM((LANES, TQ), F32),
            pltpu.VMEM((HEAD_DIM, N_CMP_PAD), BF16),
        ],
        compiler_params=pltpu.CompilerParams(
            dimension_semantics=("parallel", "parallel", "arbitrary"), vmem_limit_bytes=VMEM_LIMIT_BYTES),
        name="nsa_attention",
    )(proj, kc, vc, proj, proj, proj, proj, small, qn, kn12, bias_tiles, far_vec, band, ovt)


def _outproj_kernel(x_ref, yg_ref, yn_ref, cu_ref, cb_ref, cc_ref, cup_ref, ccp_ref, sw_ref,
                    wg_ref, wn_ref, wc_ref, o_ref, *, tiles_per_seq):
    i = pl.program_id(0)
    v = cc_ref[...].astype(F32) * cu_ref[...].astype(F32)
    vp = (ccp_ref[...].astype(F32) * cup_ref[...].astype(F32))[BF16_ROWS - SUBLANES:, :]
    vp = jnp.where(i % tiles_per_seq == 0, 0.0, vp)
    sw = sw_ref[...]
    y = v * sw[2:3, :]
    for s in range(1, SHORT_CONV_W):
        y = y + _shift_rows(v, vp, s) * sw[2 - s:3 - s, :]
    yc = (cb_ref[...].astype(F32) * y).astype(BF16)
    o_ref[...] = (x_ref[...] + _dot(yg_ref[...], wg_ref[...]) + _dot(yn_ref[...], wn_ref[...])
                  + _dot(yc, wc_ref[...]))


def _outproj(x2, y_gdn, y_nsa, proj, sconv_t, w_out, layer, seq, tm=512):
    T, D = x2.shape
    cb = CONV_DIM // LANES
    prev = lambda blk: (lambda i: (jnp.maximum(i * (tm // BF16_ROWS) - 1, 0), blk // cb))
    return pl.pallas_call(
        functools.partial(_outproj_kernel, tiles_per_seq=seq // tm),
        grid=(T // tm,),
        in_specs=[
            pl.BlockSpec((tm, D), lambda i: (i, 0)),
            pl.BlockSpec((tm, GDN_DIM), lambda i: (i, 0)),
            pl.BlockSpec((tm, NSA_DIM), lambda i: (i, 0)),
            pl.BlockSpec((tm, CONV_DIM), lambda i: (i, BLK_CU // cb)),
            pl.BlockSpec((tm, CONV_DIM), lambda i: (i, BLK_CB // cb)),
            pl.BlockSpec((tm, CONV_DIM), lambda i: (i, BLK_CC // cb)),
            pl.BlockSpec((BF16_ROWS, CONV_DIM), prev(BLK_CU)),
            pl.BlockSpec((BF16_ROWS, CONV_DIM), prev(BLK_CC)),
            pl.BlockSpec((None, SHORT_CONV_W, CONV_DIM), lambda i: (layer, 0, 0)),
            pl.BlockSpec((None, GDN_DIM, D), lambda i: (layer, 0, 0)),
            pl.BlockSpec((None, NSA_DIM, D), lambda i: (layer, GDN_DIM // NSA_DIM, 0)),
            pl.BlockSpec((None, CONV_DIM, D), lambda i: (layer, (GDN_DIM + NSA_DIM) // CONV_DIM, 0)),
        ],
        out_specs=pl.BlockSpec((tm, D), lambda i: (i, 0)),
        out_shape=jax.ShapeDtypeStruct((T, D), F32),
        compiler_params=pltpu.CompilerParams(
            dimension_semantics=("parallel",), vmem_limit_bytes=VMEM_LIMIT_BYTES),
        name="outproj",
    )(x2, y_gdn, y_nsa, proj, proj, proj, proj, proj, sconv_t, w_out, w_out, w_out)


def _ffn_kernel(x_ref, g_ref, wg_ref, wu_ref, wd_ref, o_ref, h_ref):
    @pl.when(pl.program_id(1) == 0)
    def _():
        x = x_ref[...]
        ms = jnp.mean(x * x, axis=-1, keepdims=True)
        h_ref[...] = (x * lax.rsqrt(ms + RMS_EPS) * g_ref[...]).astype(BF16)
        o_ref[...] = x

    h = h_ref[...]
    a = _dot(h, wg_ref[...])
    b = _dot(h, wu_ref[...])
    o_ref[...] += _dot((_silu(a) * b).astype(BF16), wd_ref[...])


def _ffn(x2, g, w_gate, w_up, w_down, layer, tm=1024, tf=512):
    T, D = x2.shape
    F = w_gate.shape[2]
    return pl.pallas_call(
        _ffn_kernel,
        grid=(T // tm, F // tf),
        in_specs=[
            pl.BlockSpec((tm, D), lambda i, j: (i, 0)),
            pl.BlockSpec((1, D), lambda i, j: (0, 0)),
            pl.BlockSpec((None, D, tf), lambda i, j: (layer, 0, j)),
            pl.BlockSpec((None, D, tf), lambda i, j: (layer, 0, j)),
            pl.BlockSpec((None, tf, D), lambda i, j: (layer, j, 0)),
        ],
        out_specs=pl.BlockSpec((tm, D), lambda i, j: (i, 0)),
        out_shape=jax.ShapeDtypeStruct((T, D), F32),
        scratch_shapes=[pltpu.VMEM((tm, D), BF16)],
        compiler_params=pltpu.CompilerParams(
            dimension_semantics=("parallel", "arbitrary"), vmem_limit_bytes=VMEM_LIMIT_BYTES),
        name="ffn",
    )(x2, g, w_gate, w_up, w_down)


def _t5_bucket_np(dist):
    n = np.maximum(dist, 0)
    max_exact = NUM_BUCKETS // 2
    nf = np.maximum(n, 1).astype(np.float32)
    large = max_exact + (np.log(nf / np.float32(max_exact)) / np.float32(math.log(MAX_DISTANCE / max_exact))
                         * np.float32(NUM_BUCKETS - max_exact)).astype(np.int32)
    large = np.minimum(large, NUM_BUCKETS - 1)
    return np.where(n < max_exact, n, large).astype(np.int32)


def _bias_lookup_kernel(rb_ref, idx_ref, o_ref):
    idx = idx_ref[...]
    tq = idx.shape[1]
    for h in range(NSA_HEADS):
        acc = jnp.zeros(idx.shape, F32)
        for b in range(NUM_BUCKETS):
            acc = jnp.where(idx == b, rb_ref[b * NSA_HEADS + h] * LOG2E, acc)
        g = h % NSA_GROUP
        o_ref[h // NSA_GROUP, :, g * tq:(g + 1) * tq] = acc


def _bias_lookup(rel_bias, idx):
    rows, tq = idx.shape
    return pl.pallas_call(
        _bias_lookup_kernel,
        in_specs=[pl.BlockSpec(memory_space=pltpu.SMEM),
                  pl.BlockSpec((rows, tq), lambda: (0, 0))],
        out_specs=pl.BlockSpec((NSA_KV_HEADS, rows, NSA_GROUP * tq), lambda: (0, 0, 0)),
        out_shape=jax.ShapeDtypeStruct((NSA_KV_HEADS, rows, NSA_GROUP * tq), F32),
        compiler_params=pltpu.CompilerParams(vmem_limit_bytes=VMEM_LIMIT_BYTES),
        name="bias_lookup",
    )(rel_bias.astype(F32).reshape(-1), jnp.asarray(idx))


def _nsa_tables(rel_bias, seq):
    TQ, TK, G = NSA_TQ, NSA_TK, NSA_GROUP
    j = np.arange(TK)[:, None]
    i = np.arange(TQ)[None, :]
    r = np.arange(_band_zero(seq) + N_CMP_PAD)[:, None]
    buckets = [_t5_bucket_np(0 * TK + i - j), _t5_bucket_np(1 * TK + i - j),
               np.full((SUBLANES, TQ), NUM_BUCKETS - 1, np.int32),
               _t5_bucket_np(i - (D_CMP * (r - _band_zero(seq)) + L_CMP - 1))]
    table = _bias_lookup(rel_bias, np.concatenate(buckets, axis=0).astype(np.int32))
    tiles = table[:, :2 * TK].reshape(NSA_KV_HEADS, 2, TK, G * TQ)
    far = table[:, 2 * TK:2 * TK + 1]
    band = table[:, 2 * TK + SUBLANES:]
    n_cmp = (seq - L_CMP) // D_CMP + 1
    n_sel = seq // L_SLC
    cmp_start = np.arange(n_cmp) * D_CMP
    sel_start = np.arange(n_sel) * L_SLC
    overlap = ((cmp_start[:, None] < sel_start[None, :] + L_SLC)
               & (cmp_start[:, None] + L_CMP > sel_start[None, :])).astype(np.float32)
    ovt = np.zeros((n_sel, N_CMP_PAD), np.float32)
    ovt[:, :n_cmp] = overlap.T
    return tiles, far, band, jnp.asarray(ovt)


def _permute_w_in(w_in):
    L, D, _ = w_in.shape
    o = np.cumsum([0, 3 * GDN_DIM, GDN_DIM, GDN_HEADS, GDN_HEADS, NSA_DIM, 6 * NSA_KV_HEADS * HEAD_DIM,
                   3 * NSA_HEADS, CONV_DIM, CONV_DIM, CONV_DIM])
    seg = lambda a, b: w_in[:, :, o[a]:o[b]]
    zeros = lambda n: jnp.zeros((L, D, n), w_in.dtype)
    n_small = 2 * GDN_HEADS + 3 * NSA_HEADS
    parts = [seg(0, 2),
             seg(4, 6),
             seg(2, 4), seg(6, 7), zeros(LANES - n_small),
             zeros(LANES),
             seg(7, 10)]
    w = jnp.concatenate(parts, axis=-1)
    assert w.shape[-1] == PROJ_PAD
    return w.astype(BF16)


def kernel(x, rel_bias, norm_mix, w_in, gdn_conv, gdn_a_log, gdn_dt_bias, gdn_norm, nsa_q_norm, nsa_k_norm,
           cmp_pos, cmp_w1, cmp_w2, sconv_w, w_out, norm_ffn, w_gate, w_up, w_down):
    B, S, D = x.shape
    depth = w_in.shape[0]
    T = B * S
    x2 = x.reshape(T, D)

    w_in_p = _permute_w_in(w_in)
    w_out_b = w_out.astype(BF16)
    w_gate_b = w_gate.astype(BF16)
    w_up_b = w_up.astype(BF16)
    w_down_b = w_down.astype(BF16)
    cmp_w1_b = cmp_w1.astype(BF16)
    cmp_w2_b = cmp_w2.astype(BF16)
    conv_t = jnp.transpose(gdn_conv, (0, 2, 1))
    sconv_t = jnp.transpose(sconv_w, (0, 2, 1))
    pad_a = lambda v: jnp.pad(v, ((0, 0), (LANE_A, LANES - LANE_A - GDN_HEADS)))[:, None, :]
    alog_vec = pad_a(gdn_a_log)
    dt_vec = pad_a(gdn_dt_bias)
    bias_tiles, far_vec, band, ovt = _nsa_tables(rel_bias, S)

    for l in range(depth):
        proj, small = _inproj(x2, norm_mix[l][None, :], w_in_p, l)
        y_gdn = _gdn(proj, small, conv_t, alog_vec[l], dt_vec[l], gdn_norm[l][None, :], l, B, S)
        kc, vc = _compress(proj, cmp_pos, cmp_w1_b, cmp_w2_b, nsa_k_norm[l, 0][None, :], l, B, S)
        y_nsa = _nsa(proj, small, kc, vc, nsa_q_norm[l][None, :], nsa_k_norm[l, 1:3], bias_tiles, far_vec,
                     band, ovt, B, S)
        x2 = _outproj(x2, y_gdn, y_nsa, proj, sconv_t, w_out_b, l, S)
        x2 = _ffn(x2, norm_ffn[l][None, :], w_gate_b, w_up_b, w_down_b, l)
    return x2.reshape(B, S, D)
```
